```python
import jax, jax.numpy as jnp
from jax import lax
import numpy as np

D_MODEL = 1024
BATCH = 8
SEQ = 4096
DEPTH = 1
DEC_BATCH = 16
DEC_SEQ = 4096
PAST_LEN = 128

A_HEADS = 8
A_KV_HEADS = 2
A_HEAD_DIM = 64
WINDOW = 128
BLOCK = 128
B_HEADS = 8
Q_LORA = 256
KV_LORA = 128
QK_NOPE = 64
QK_ROPE = 32
V_HEAD = 64
D_FF = 2816
CONV_W = 3
ROPE_THETA = 10000.0
EPS = 1e-6
NEG_INF = -1e30

A_Q = A_HEADS * A_HEAD_DIM
A_KV = A_KV_HEADS * A_HEAD_DIM
B_QK = QK_NOPE + QK_ROPE
IN_SPLITS = (A_Q, A_Q + A_KV, A_Q + 2 * A_KV, A_Q + 2 * A_KV + Q_LORA, A_Q + 2 * A_KV + Q_LORA + KV_LORA, A_Q + 2 * A_KV + Q_LORA + KV_LORA + QK_ROPE, A_Q + 2 * A_KV + Q_LORA + KV_LORA + QK_ROPE + D_MODEL)
IN_TOTAL = A_Q + 2 * A_KV + Q_LORA + KV_LORA + QK_ROPE + 2 * D_MODEL

kernel_name = 'hybrid_swa_mla_convffn_adaln_encoder'


def _rmsnorm(x, g):
    xf = x.astype(jnp.float32)
    y = xf * lax.rsqrt(jnp.mean(xf * xf, axis=-1, keepdims=True) + EPS)
    return y.astype(x.dtype) * g


def _rope_tables(seq, dim, dtype):
    inv = 1.0 / (ROPE_THETA ** (jnp.arange(0, dim, 2, dtype=jnp.float32) / dim))
    ang = jnp.arange(seq, dtype=jnp.float32)[:, None] * inv[None, :]
    return jnp.cos(ang).astype(dtype), jnp.sin(ang).astype(dtype)


def _apply_rope(x, cos, sin):
    c = cos[:, None, :]
    s = sin[:, None, :]
    x1, x2 = jnp.split(x, 2, axis=-1)
    return jnp.concatenate([x1 * c - x2 * s, x1 * s + x2 * c], axis=-1)


def _band(t):
    b, s, hk, d = t.shape
    nb = s // BLOCK
    tp = jnp.pad(t, ((0, 0), (BLOCK, BLOCK), (0, 0), (0, 0))).reshape(b, nb + 2, BLOCK, hk, d)
    return jnp.concatenate([tp[:, :-2], tp[:, 1:-1], tp[:, 2:]], axis=2)


def _window_attention(q, k, v, sink):
    b, s, h, dh = q.shape
    hkv = k.shape[2]
    grp = h // hkv
    nb = s // BLOCK
    qb = q.reshape(b, nb, BLOCK, hkv, grp, dh)
    kb = _band(k)
    vb = _band(v)
    sc = jnp.einsum('bnqhgd,bnkhd->bnhgqk', qb, kb).astype(jnp.float32) * (dh ** -0.5)
    qpos = jnp.arange(nb)[:, None] * BLOCK + jnp.arange(BLOCK)[None, :]
    kpos = (jnp.arange(nb)[:, None] - 1) * BLOCK + jnp.arange(3 * BLOCK)[None, :]
    valid = ((jnp.abs(qpos[:, :, None] - kpos[:, None, :]) <= WINDOW)
             & (kpos[:, None, :] >= 0) & (kpos[:, None, :] < s))
    sc = jnp.where(valid[None, :, None, None], sc, NEG_INF)
    sk = sink.astype(jnp.float32).reshape(1, 1, hkv, grp, 1, 1)
    m = jnp.maximum(jnp.max(sc, axis=-1, keepdims=True), sk)
    p = jnp.exp(sc - m)
    p = p / (jnp.sum(p, axis=-1, keepdims=True) + jnp.exp(sk - m))
    o = jnp.einsum('bnhgqk,bnkhd->bnqhgd', p.astype(v.dtype), vb)
    return o.reshape(b, s, h * dh)


def _dense_attention_blocks(q, k, v):
    b, s, h, dq = q.shape
    dv = v.shape[-1]
    nb = s // BLOCK
    qb = jnp.moveaxis(q.reshape(b, nb, BLOCK, h, dq), 1, 0)
    scale = dq ** -0.5

    def one_block(qi):
        sc = jnp.einsum('bqhd,bkhd->bhqk', qi, k).astype(jnp.float32) * scale
        p = jax.nn.softmax(sc, axis=-1)
        return jnp.einsum('bhqk,bkhd->bqhd', p.astype(v.dtype), v)

    o = lax.map(one_block, qb)
    return jnp.moveaxis(o, 0, 1).reshape(b, s, h * dv)


def _dwconv(u, w, bias):
    s = u.shape[1]
    half = CONV_W // 2
    up = jnp.pad(u, ((0, 0), (half, half), (0, 0)))
    out = bias
    for j in range(CONV_W):
        out = out + up[:, j:j + s] * w[j]
    return out


def _layer(x, c, w_ada, b_ada, g_attn, w_in, gq_a, gk_a, sink_a, g_cq, w_uq, g_ckv, w_ukv,
           gq_b, gk_b, w_o_a, w_o_b, w_out, g_ffn, w_up, conv_w, conv_b, w_down):
    b, s, _ = x.shape
    mod = jax.nn.silu(c) @ w_ada + b_ada
    sh1, sc1, gt1, sh2, sc2, gt2 = [t[:, None, :] for t in jnp.split(mod, 6, axis=-1)]

    h = _rmsnorm(x, g_attn) * (1 + sc1) + sh1
    z = h @ w_in
    qa, ka, va, cq, ckv, kr, gla, glb = jnp.split(z, IN_SPLITS, axis=-1)

    cos_a, sin_a = _rope_tables(s, A_HEAD_DIM, x.dtype)
    qa = _apply_rope(_rmsnorm(qa.reshape(b, s, A_HEADS, A_HEAD_DIM), gq_a), cos_a, sin_a)
    ka = _apply_rope(_rmsnorm(ka.reshape(b, s, A_KV_HEADS, A_HEAD_DIM), gk_a), cos_a, sin_a)
    va = va.reshape(b, s, A_KV_HEADS, A_HEAD_DIM)
    oa = _window_attention(qa, ka, va, sink_a)

    qb = (_rmsnorm(cq, g_cq) @ w_uq).reshape(b, s, B_HEADS, B_QK)
    kv = (_rmsnorm(ckv, g_ckv) @ w_ukv).reshape(b, s, B_HEADS, QK_NOPE + V_HEAD)
    k_nope = kv[..., :QK_NOPE]
    vb = kv[..., QK_NOPE:]
    kb = jnp.concatenate([k_nope, jnp.broadcast_to(kr[:, :, None, :], (b, s, B_HEADS, QK_ROPE))], axis=-1)
    qb = _rmsnorm(qb, gq_b)
    kb = _rmsnorm(kb, gk_b)
    cos_b, sin_b = _rope_tables(s, QK_ROPE, x.dtype)
    qb = jnp.concatenate([qb[..., :QK_NOPE], _apply_rope(qb[..., QK_NOPE:], cos_b, sin_b)], axis=-1)
    kb = jnp.concatenate([kb[..., :QK_NOPE], _apply_rope(kb[..., QK_NOPE:], cos_b, sin_b)], axis=-1)
    ob = _dense_attention_blocks(qb, kb, vb)

    mix = jax.nn.sigmoid(gla) * (oa @ w_o_a) + jax.nn.sigmoid(glb) * (ob @ w_o_b)
    x = x + gt1 * (mix @ w_out)

    h2 = _rmsnorm(x, g_ffn) * (1 + sc2) + sh2
    u = _dwconv(h2 @ w_up, conv_w, conv_b)
    ua, ug = jnp.split(u, 2, axis=-1)
    x = x + gt2 * ((jax.nn.silu(ug) * ua) @ w_down)
    return x


def setup_inputs(seed: int = 0) -> dict:
    key = jax.random.key(seed)
    ks = jax.random.split(key, 32)
    f32 = jnp.float32

    def nrm(k, shape, fan_in):
        return jax.random.normal(k, shape, f32) * (fan_in ** -0.5)

    def gain(k, shape):
        return 1.0 + 0.1 * jax.random.normal(k, shape, f32)

    def small(k, shape, scale):
        return scale * jax.random.normal(k, shape, f32)

    L = DEPTH
    return {
        'x_prompt': jax.random.normal(ks[0], (BATCH, SEQ, D_MODEL), f32),
        'x_sample': jax.random.normal(ks[1], (DEC_BATCH, DEC_SEQ, D_MODEL), f32),
        'c_prompt': jax.random.normal(ks[2], (BATCH, D_MODEL), f32),
        'c_sample': jax.random.normal(ks[3], (DEC_BATCH, D_MODEL), f32),
        'w_ada': nrm(ks[4], (L, D_MODEL, 6 * D_MODEL), D_MODEL),
        'b_ada': small(ks[5], (L, 6 * D_MODEL), 0.02),
        'g_attn': gain(ks[6], (L, D_MODEL)),
        'w_in': nrm(ks[7], (L, D_MODEL, IN_TOTAL), D_MODEL),
        'gq_a': gain(ks[8], (L, A_HEAD_DIM)),
        'gk_a': gain(ks[9], (L, A_HEAD_DIM)),
        'sink_a': small(ks[10], (L, A_HEADS), 0.5),
        'g_cq': gain(ks[11], (L, Q_LORA)),
        'w_uq': nrm(ks[12], (L, Q_LORA, B_HEADS * B_QK), Q_LORA),
        'g_ckv': gain(ks[13], (L, KV_LORA)),
        'w_ukv': nrm(ks[14], (L, KV_LORA, B_HEADS * (QK_NOPE + V_HEAD)), KV_LORA),
        'gq_b': gain(ks[15], (L, B_QK)),
        'gk_b': gain(ks[16], (L, B_QK)),
        'w_o_a': nrm(ks[17], (L, A_Q, D_MODEL), A_Q),
        'w_o_b': nrm(ks[18], (L, B_HEADS * V_HEAD, D_MODEL), B_HEADS * V_HEAD),
        'w_out': nrm(ks[19], (L, D_MODEL, D_MODEL), D_MODEL),
        'g_ffn': gain(ks[20], (L, D_MODEL)),
        'w_up': nrm(ks[21], (L, D_MODEL, 2 * D_FF), D_MODEL),
        'conv_w': nrm(ks[22], (L, CONV_W, 2 * D_FF), CONV_W),
        'conv_b': small(ks[23], (L, 2 * D_FF), 0.02),
        'w_down': nrm(ks[24], (L, D_FF, D_MODEL), D_FF),
    }


def reference(x_prompt, x_sample, c_prompt, c_sample, w_ada, b_ada, g_attn, w_in, gq_a, gk_a,
              sink_a, g_cq, w_uq, g_ckv, w_ukv, gq_b, gk_b, w_o_a, w_o_b, w_out, g_ffn, w_up,
              conv_w, conv_b, w_down):
    y_prompt = x_prompt
    y_sample = x_sample
    for l in range(DEPTH):
        p = (w_ada[l], b_ada[l], g_attn[l], w_in[l], gq_a[l], gk_a[l], sink_a[l], g_cq[l], w_uq[l],
             g_ckv[l], w_ukv[l], gq_b[l], gk_b[l], w_o_a[l], w_o_b[l], w_out[l], g_ffn[l], w_up[l],
             conv_w[l], conv_b[l], w_down[l])
        y_prompt = _layer(y_prompt, c_prompt, *p)
        y_sample = _layer(y_sample, c_sample, *p)
    return (y_prompt, y_sample)
```

```python
import functools

import jax
import jax.numpy as jnp
from jax import lax
from jax.experimental import pallas as pl
from jax.experimental.pallas import tpu as pltpu

F32 = jnp.float32
BF16 = jnp.bfloat16

D_MODEL = 1024
A_HEADS = 8
A_KV_HEADS = 2
A_HEAD_DIM = 64
WINDOW = 128
B_HEADS = 8
Q_LORA = 256
KV_LORA = 128
QK_NOPE = 64
QK_ROPE = 32
V_HEAD = 64
D_FF = 2816
ROPE_THETA = 10000.0
EPS = 1e-6
NEG_INF = -1e30

A_Q = A_HEADS * A_HEAD_DIM
A_KV = A_KV_HEADS * A_HEAD_DIM
B_QK = QK_NOPE + QK_ROPE
LANES = 128
B_PAD = B_HEADS * LANES
VT_ROWS = 80

C_QA, C_KA, C_VA, C_CQ, C_CKV, C_KR, C_GA, C_GB, C_END = 0, 512, 640, 768, 1024, 1152, 1280, 2304, 3328

VMEM_LIMIT = 56 * 1024 * 1024


def _cparams(n_axes):
    return pltpu.CompilerParams(dimension_semantics=("arbitrary",) * n_axes, vmem_limit_bytes=VMEM_LIMIT)


def _dot(a, b):
    return jnp.dot(a, b, preferred_element_type=F32)


def _dot_nt(a, b):
    return lax.dot_general(a, b, (((1,), (1,)), ((), ())), preferred_element_type=F32)


def _mod_kernel(c_ref, w_ref, b_ref, o_ref):
    c = c_ref[...]
    sc = c * jax.nn.sigmoid(c)
    o_ref[...] = _dot(sc.astype(BF16), w_ref[...].astype(BF16)) + b_ref[...]


def _modulation(c, w_ada, b_ada):
    b = c.shape[0]
    n = w_ada.shape[1]
    tn = 1024
    return pl.pallas_call(
        _mod_kernel,
        grid=(n // tn,),
        in_specs=[
            pl.BlockSpec((b, D_MODEL), lambda j: (0, 0)),
            pl.BlockSpec((D_MODEL, tn), lambda j: (0, j)),
            pl.BlockSpec((1, tn), lambda j: (0, j)),
        ],
        out_specs=pl.BlockSpec((b, tn), lambda j: (0, j)),
        out_shape=jax.ShapeDtypeStruct((b, n), F32),
        compiler_params=_cparams(1),
        name="adaln_mod",
    )(c, w_ada, b_ada.reshape(1, n))


def _group_norm(x, bd, inv_n, gain):
    ss = _dot((x * x).astype(BF16), bd)
    return x * lax.rsqrt(ss * inv_n + EPS) * gain


def _rope(x, c, s_up, s_dn, shift):
    return x * c + pltpu.roll(x, LANES - shift, 1) * s_up + pltpu.roll(x, shift, 1) * s_dn


def _pre_kernel(x_ref, mod_ref, g_ref, w_in_ref, w_uq_ref, w_k_ref, w_v_ref, gcq_ref, gckv_ref,
                gqa_ref, gka_ref, gqb_ref, gkb_ref, bd64_ref, bd128_ref,
                ca_ref, sua_ref, sda_ref, cb_ref, sub_ref, sdb_ref,
                qa_o, ka_o, va_o, qb_o, kb_o, vt_o, sga_o, sgb_o):
    x = x_ref[0]
    sh1 = mod_ref[0, 0:1, :]
    sc1 = mod_ref[0, 1:2, :]
    h = x * lax.rsqrt(jnp.mean(x * x, axis=-1, keepdims=True) + EPS) * g_ref[...]
    hb = (h * (1.0 + sc1) + sh1).astype(BF16)

    ca, sua, sda = ca_ref[...], sua_ref[...], sda_ref[...]
    cb, sub, sdb = cb_ref[...], sub_ref[...], sdb_ref[...]
    bd64 = bd64_ref[...]
    bd128 = bd128_ref[...]

    zq = _dot(hb, w_in_ref[:, C_QA:C_KA])
    for c in range(2):
        xn = _group_norm(zq[:, 256 * c:256 * (c + 1)], bd64, 1.0 / A_HEAD_DIM, gqa_ref[:, 256 * c:256 * (c + 1)])
        for j in range(2):
            lo = 256 * c + LANES * j
            qa_o[0, :, lo:lo + LANES] = _rope(xn[:, LANES * j:LANES * (j + 1)], ca, sua, sda, 32).astype(BF16)
    zk = _dot(hb, w_in_ref[:, C_KA:C_VA])
    ssk = _dot((zk * zk).astype(BF16), bd64[0:LANES, 0:LANES])
    kn = zk * lax.rsqrt(ssk * (1.0 / A_HEAD_DIM) + EPS) * gka_ref[...]
    ka_o[0] = _rope(kn, ca, sua, sda, 32).astype(BF16)
    va_o[0] = _dot(hb, w_in_ref[:, C_VA:C_CQ]).astype(BF16)

    cq = _dot(hb, w_in_ref[:, C_CQ:C_CKV])
    cqn = (cq * lax.rsqrt(jnp.mean(cq * cq, axis=-1, keepdims=True) + EPS) * gcq_ref[...]).astype(BF16)
    ckv = _dot(hb, w_in_ref[:, C_CKV:C_KR])
    ckvn = (ckv * lax.rsqrt(jnp.mean(ckv * ckv, axis=-1, keepdims=True) + EPS) * gckv_ref[...]).astype(BF16)
    krp = _dot(hb, w_in_ref[:, C_KR:C_GA]).astype(BF16)

    qraw = _dot(cqn, w_uq_ref[...])
    kraw = _dot(jnp.concatenate([ckvn, krp], axis=1), w_k_ref[...])
    for c in range(4):
        sl = slice(256 * c, 256 * (c + 1))
        qn = _group_norm(qraw[:, sl], bd128, 1.0 / B_QK, gqb_ref[:, sl])
        kn2 = _group_norm(kraw[:, sl], bd128, 1.0 / B_QK, gkb_ref[:, sl])
        for j in range(2):
            hh = 2 * c + j
            ls = slice(LANES * j, LANES * (j + 1))
            qb_o[0, hh] = _rope(qn[:, ls], cb, sub, sdb, 16).astype(BF16)
            kb_o[0, hh] = _rope(kn2[:, ls], cb, sub, sdb, 16).astype(BF16)

    vv = _dot(ckvn, w_v_ref[...])
    vt = vv.T
    t = vt.shape[1]
    row = lax.broadcasted_iota(jnp.int32, (VT_ROWS - V_HEAD, t), 0)
    tail = jnp.where(row == 0, 1.0, 0.0).astype(BF16)
    for hh in range(B_HEADS):
        vt_o[0, hh, 0, 0:V_HEAD, :] = vt[V_HEAD * hh:V_HEAD * (hh + 1), :].astype(BF16)
        vt_o[0, hh, 0, V_HEAD:VT_ROWS, :] = tail

    sga_o[0] = jax.nn.sigmoid(_dot(hb, w_in_ref[:, C_GA:C_GB])).astype(BF16)
    sgb_o[0] = jax.nn.sigmoid(_dot(hb, w_in_ref[:, C_GB:C_END])).astype(BF16)


def _pre_attention(x, mod, w, t):
    b, s, _ = x.shape
    nt = s // t
    const2 = lambda i, bb: (0, 0)
    tok2 = lambda i, bb: (i, 0)
    in_specs = [
        pl.BlockSpec((1, t, D_MODEL), lambda i, bb: (bb, i, 0)),
        pl.BlockSpec((1, 6, D_MODEL), lambda i, bb: (bb, 0, 0)),
        pl.BlockSpec((1, D_MODEL), const2),
        pl.BlockSpec((D_MODEL, C_END), const2),
        pl.BlockSpec((Q_LORA, B_PAD), const2),
        pl.BlockSpec((256, B_PAD), const2),
        pl.BlockSpec((KV_LORA, B_HEADS * V_HEAD), const2),
        pl.BlockSpec((1, Q_LORA), const2),
        pl.BlockSpec((1, KV_LORA), const2),
        pl.BlockSpec((1, A_Q), const2),
        pl.BlockSpec((1, A_KV), const2),
        pl.BlockSpec((1, B_PAD), const2),
        pl.BlockSpec((1, B_PAD), const2),
        pl.BlockSpec((256, 256), const2),
        pl.BlockSpec((256, 256), const2),
    ] + [pl.BlockSpec((t, LANES), tok2)] * 6
    out_shape = [
        jax.ShapeDtypeStruct((b, s, A_Q), BF16),
        jax.ShapeDtypeStruct((b, s, A_KV), BF16),
        jax.ShapeDtypeStruct((b, s, A_KV), BF16),
        jax.ShapeDtypeStruct((b, B_HEADS, s, LANES), BF16),
        jax.ShapeDtypeStruct((b, B_HEADS, s, LANES), BF16),
        jax.ShapeDtypeStruct((b, B_HEADS, nt, VT_ROWS, t), BF16),
        jax.ShapeDtypeStruct((b, s, D_MODEL), BF16),
        jax.ShapeDtypeStruct((b, s, D_MODEL), BF16),
    ]
    tok3 = lambda i, bb: (bb, i, 0)
    out_specs = [
        pl.BlockSpec((1, t, A_Q), tok3),
        pl.BlockSpec((1, t, A_KV), tok3),
        pl.BlockSpec((1, t, A_KV), tok3),
        pl.BlockSpec((1, B_HEADS, t, LANES), lambda i, bb: (bb, 0, i, 0)),
        pl.BlockSpec((1, B_HEADS, t, LANES), lambda i, bb: (bb, 0, i, 0)),
        pl.BlockSpec((1, B_HEADS, 1, VT_ROWS, t), lambda i, bb: (bb, 0, i, 0, 0)),
        pl.BlockSpec((1, t, D_MODEL), tok3),
        pl.BlockSpec((1, t, D_MODEL), tok3),
    ]
    return pl.pallas_call(
        _pre_kernel,
        grid=(nt, b),
        in_specs=in_specs,
        out_specs=out_specs,
        out_shape=out_shape,
        compiler_params=_cparams(2),
        name="pre_attention",
    )(x, mod, w["g_attn"], w["w_in"], w["w_uq"], w["w_k"], w["w_v"], w["g_cq"], w["g_ckv"],
      w["gq_a"], w["gk_a"], w["gq_b"], w["gk_b"], w["bd64"], w["bd128"],
      w["ca"], w["sua"], w["sda"], w["cb"], w["sub"], w["sdb"])


def _window_kernel(seq_len, tq, sink_ref, q_ref, kp_ref, km_ref, kn_ref, vp_ref, vm_ref, vn_ref, o_ref,
                   kcat, vcat):
    i = pl.program_id(1)
    blk = WINDOW
    kcat[0:blk] = kp_ref[0]
    kcat[blk:blk + tq] = km_ref[0]
    kcat[blk + tq:blk + tq + blk] = kn_ref[0]
    vcat[0:blk] = vp_ref[0]
    vcat[blk:blk + tq] = vm_ref[0]
    vcat[blk + tq:blk + tq + blk] = vn_ref[0]
    grp = A_HEADS // A_KV_HEADS
    for j in range(tq // blk):
        kb = kcat[j * blk:(j + 3) * blk]
        vb = vcat[j * blk:(j + 3) * blk]
        qpos = i * tq + j * blk + lax.broadcasted_iota(jnp.int32, (blk, 3 * blk), 0)
        kpos = i * tq + (j - 1) * blk + lax.broadcasted_iota(jnp.int32, (blk, 3 * blk), 1)
        valid = (jnp.abs(qpos - kpos) <= WINDOW) & (kpos >= 0) & (kpos < seq_len)
        for hk in range(A_KV_HEADS):
            kh = kb[:, hk * A_HEAD_DIM:(hk + 1) * A_HEAD_DIM]
            vh = vb[:, hk * A_HEAD_DIM:(hk + 1) * A_HEAD_DIM]
            for g in range(grp):
                hq = hk * grp + g
                qh = q_ref[0, j * blk:(j + 1) * blk, hq * A_HEAD_DIM:(hq + 1) * A_HEAD_DIM]
                sc = jnp.where(valid, _dot_nt(qh, kh), NEG_INF)
                sk = sink_ref[hq]
                m = jnp.maximum(jnp.max(sc, axis=-1, keepdims=True), sk)
                p = jnp.exp(sc - m)
                den = jnp.sum(p, axis=-1, keepdims=True) + jnp.exp(sk - m)
                o = _dot(p.astype(BF16), vh) / den
                o_ref[0, j * blk:(j + 1) * blk, hq * A_HEAD_DIM:(hq + 1) * A_HEAD_DIM] = o.astype(BF16)


def _window_attention(qa, ka, va, sink, tq):
    b, s, _ = qa.shape
    r = tq // WINDOW
    nblk = s // WINDOW
    main = lambda bb, i: (bb, i, 0)
    prev = lambda bb, i: (bb, jnp.maximum(i * r - 1, 0), 0)
    nxt = lambda bb, i: (bb, jnp.minimum((i + 1) * r, nblk - 1), 0)
    kv_specs = [
        pl.BlockSpec((1, WINDOW, A_KV), prev),
        pl.BlockSpec((1, tq, A_KV), main),
        pl.BlockSpec((1, WINDOW, A_KV), nxt),
    ]
    return pl.pallas_call(
        functools.partial(_window_kernel, s, tq),
        grid=(b, s // tq),
        in_specs=[pl.BlockSpec(memory_space=pltpu.SMEM), pl.BlockSpec((1, tq, A_Q), main)] + kv_specs + kv_specs,
        out_specs=pl.BlockSpec((1, tq, A_Q), main),
        out_shape=jax.ShapeDtypeStruct((b, s, A_Q), BF16),
        scratch_shapes=[pltpu.VMEM((tq + 2 * WINDOW, A_KV), BF16), pltpu.VMEM((tq + 2 * WINDOW, A_KV), BF16)],
        compiler_params=_cparams(2),
        name="window_attention",
    )(sink, qa, ka, ka, ka, va, va, va)


def _mla_kernel(nc, q_ref, k_ref, vt_ref, o_ref, s_scr, o_scr):
    tk = vt_ref.shape[-1]
    tq = q_ref.shape[2]

    def head(h, carry):
        q = q_ref[0, h]

        def pass1(c, m):
            kc = k_ref[0, h, pl.ds(pl.multiple_of(c * tk, tk), tk), :]
            st = _dot_nt(kc, q)
            s_scr[c] = st
            return jnp.maximum(m, jnp.max(st, axis=0, keepdims=True))

        m = lax.fori_loop(0, nc, pass1, jnp.full((1, tq), -jnp.inf, F32))

        def pass2(c, acc):
            p = jnp.exp(s_scr[c] - m).astype(BF16)
            return acc + _dot(vt_ref[0, h, c], p)

        acc = lax.fori_loop(0, nc, pass2, jnp.zeros((VT_ROWS, tq), F32))
        o_scr[pl.ds(pl.multiple_of(h * V_HEAD, V_HEAD), V_HEAD), :] = acc[0:V_HEAD] / acc[V_HEAD:V_HEAD + 1]
        return carry

    lax.fori_loop(0, B_HEADS, head, 0)
    o_ref[0] = o_scr[...].T.astype(BF16)


def _mla_attention(qb, kb, vt, tq):
    b, _, s, _ = qb.shape
    nc, tk = vt.shape[2], vt.shape[4]
    return pl.pallas_call(
        functools.partial(_mla_kernel, nc),
        grid=(b, s // tq),
        in_specs=[
            pl.BlockSpec((1, B_HEADS, tq, LANES), lambda bb, i: (bb, 0, i, 0)),
            pl.BlockSpec((1, B_HEADS, s, LANES), lambda bb, i: (bb, 0, 0, 0)),
            pl.BlockSpec((1, B_HEADS, nc, VT_ROWS, tk), lambda bb, i: (bb, 0, 0, 0, 0)),
        ],
        out_specs=pl.BlockSpec((1, tq, B_HEADS * V_HEAD), lambda bb, i: (bb, i, 0)),
        out_shape=jax.ShapeDtypeStruct((b, s, B_HEADS * V_HEAD), BF16),
        scratch_shapes=[pltpu.VMEM((nc, tk, tq), F32), pltpu.VMEM((B_HEADS * V_HEAD, tq), F32)],
        compiler_params=_cparams(2),
        name="mla_attention",
    )(qb, kb, vt)


def _merge_kernel(x_ref, mod_ref, oa_ref, ob_ref, sga_ref, sgb_ref, woa_ref, wob_ref, wout_ref, g_ref,
                  x1_o, h2_o):
    gt1 = mod_ref[0, 2:3, :]
    sh2 = mod_ref[0, 3:4, :]
    sc2 = mod_ref[0, 4:5, :]
    mix = (sga_ref[0].astype(F32) * _dot(oa_ref[0], woa_ref[...])
           + sgb_ref[0].astype(F32) * _dot(ob_ref[0], wob_ref[...]))
    x1 = x_ref[0] + gt1 * _dot(mix.astype(BF16), wout_ref[...])
    x1_o[0] = x1
    h2 = x1 * lax.rsqrt(jnp.mean(x1 * x1, axis=-1, keepdims=True) + EPS) * g_ref[...]
    h2_o[0] = (h2 * (1.0 + sc2) + sh2).astype(BF16)


def _merge(x, mod, oa, ob, sga, sgb, w, t):
    b, s, _ = x.shape
    tok = lambda i, bb: (bb, i, 0)
    const2 = lambda i, bb: (0, 0)
    return pl.pallas_call(
        _merge_kernel,
        grid=(s // t, b),
        in_specs=[
            pl.BlockSpec((1, t, D_MODEL), tok),
            pl.BlockSpec((1, 6, D_MODEL), lambda i, bb: (bb, 0, 0)),
            pl.BlockSpec((1, t, A_Q), tok),
            pl.BlockSpec((1, t, B_HEADS * V_HEAD), tok),
            pl.BlockSpec((1, t, D_MODEL), tok),
            pl.BlockSpec((1, t, D_MODEL), tok),
            pl.BlockSpec((A_Q, D_MODEL), const2),
            pl.BlockSpec((B_HEADS * V_HEAD, D_MODEL), const2),
            pl.BlockSpec((D_MODEL, D_MODEL), const2),
            pl.BlockSpec((1, D_MODEL), const2),
        ],
        out_specs=[pl.BlockSpec((1, t, D_MODEL), tok), pl.BlockSpec((1, t, D_MODEL), tok)],
        out_shape=[jax.ShapeDtypeStruct((b, s, D_MODEL), F32), jax.ShapeDtypeStruct((b, s, D_MODEL), BF16)],
        compiler_params=_cparams(2),
        name="merge_outproj",
    )(x, mod, oa, ob, sga, sgb, w["w_o_a"], w["w_o_b"], w["w_out"], w["g_ffn"])


FF_CHUNK = D_FF // 2
HALO = 16


def _ffn_kernel(x1_ref, mod_ref, hp_ref, hm_ref, hn_ref, wup_ref, cw_ref, cb_ref, wdn_ref, y_o,
                hcat, ua_scr, ug_scr):
    i = pl.program_id(0)
    last = pl.num_programs(0) - 1
    t = hm_ref.shape[1]
    gt2 = mod_ref[0, 5:6, :]
    hcat[0:HALO] = jnp.where(i > 0, hp_ref[0], jnp.zeros_like(hp_ref[0]))
    hcat[HALO:HALO + t] = hm_ref[0]
    hcat[HALO + t:HALO + t + HALO] = jnp.where(i < last, hn_ref[0], jnp.zeros_like(hn_ref[0]))
    hc = hcat[...]
    acc = jnp.zeros((t, D_MODEL), F32)
    for c in range(D_FF // FF_CHUNK):
        branch = []
        for off, scr in ((0, ua_scr), (D_FF, ug_scr)):
            lo = off + c * FF_CHUNK
            scr[...] = _dot(hc, wup_ref[:, lo:lo + FF_CHUNK])
            u = cb_ref[:, lo:lo + FF_CHUNK]
            for j in range(3):
                u = u + scr[pl.ds(HALO - 1 + j, t), :] * cw_ref[j:j + 1, lo:lo + FF_CHUNK]
            branch.append(u)
        ua, ug = branch
        act = (ug * jax.nn.sigmoid(ug) * ua).astype(BF16)
        acc = acc + _dot(act, wdn_ref[c * FF_CHUNK:(c + 1) * FF_CHUNK, :])
    y_o[0] = x1_ref[0] + gt2 * acc


def _ffn(x1, mod, h2, w, t):
    b, s, _ = x1.shape
    r = t // HALO
    nh = s // HALO
    tok = lambda i, bb: (bb, i, 0)
    const2 = lambda i, bb: (0, 0)
    return pl.pallas_call(
        _ffn_kernel,
        grid=(s // t, b),
        in_specs=[
            pl.BlockSpec((1, t, D_MODEL), tok),
            pl.BlockSpec((1, 6, D_MODEL), lambda i, bb: (bb, 0, 0)),
            pl.BlockSpec((1, HALO, D_MODEL), lambda i, bb: (bb, jnp.maximum(i * r - 1, 0), 0)),
            pl.BlockSpec((1, t, D_MODEL), tok),
            pl.BlockSpec((1, HALO, D_MODEL), lambda i, bb: (bb, jnp.minimum((i + 1) * r, nh - 1), 0)),
            pl.BlockSpec((D_MODEL, 2 * D_FF), const2),
            pl.BlockSpec((3, 2 * D_FF), const2),
            pl.BlockSpec((1, 2 * D_FF), const2),
            pl.BlockSpec((D_FF, D_MODEL), const2),
        ],
        out_specs=pl.BlockSpec((1, t, D_MODEL), tok),
        out_shape=jax.ShapeDtypeStruct((b, s, D_MODEL), F32),
        scratch_shapes=[
            pltpu.VMEM((t + 2 * HALO, D_MODEL), BF16),
            pltpu.VMEM((t + 2 * HALO, FF_CHUNK), F32),
            pltpu.VMEM((t + 2 * HALO, FF_CHUNK), F32),
        ],
        compiler_params=_cparams(2),
        name="conv_ffn",
    )(x1, mod, h2, h2, h2, w["w_up"], w["conv_w"], w["conv_b"], w["w_down"])


def _rope_tables(seq, dim, n_rep, lead, pad):
    half = dim // 2
    inv = 1.0 / (ROPE_THETA ** (jnp.arange(0, dim, 2, dtype=F32) / dim))
    ang = jnp.arange(seq, dtype=F32)[:, None] * inv[None, :]
    cos, sin = jnp.cos(ang), jnp.sin(ang)
    zero = jnp.zeros_like(sin)
    c = jnp.concatenate([jnp.ones((seq, lead), F32)] + [cos, cos] * n_rep + [jnp.zeros((seq, pad), F32)], axis=1)
    s_up = jnp.concatenate([jnp.zeros((seq, lead), F32)] + [-sin, zero] * n_rep + [jnp.zeros((seq, pad), F32)], axis=1)
    s_dn = jnp.concatenate([jnp.zeros((seq, lead), F32)] + [zero, sin] * n_rep + [jnp.zeros((seq, pad), F32)], axis=1)
    del half
    return c, s_up, s_dn


def _prepare(seq, w_ada, b_ada, g_attn, w_in, gq_a, gk_a, sink_a, g_cq, w_uq, g_ckv, w_ukv, gq_b, gk_b,
             w_o_a, w_o_b, w_out, g_ffn, w_up, conv_w, conv_b, w_down):
    kr_lo = A_Q + 2 * A_KV + Q_LORA + KV_LORA
    kr_hi = kr_lo + QK_ROPE
    w_in_p = jnp.concatenate(
        [w_in[:, :kr_hi], jnp.zeros((D_MODEL, LANES - QK_ROPE), F32), w_in[:, kr_hi:]], axis=1).astype(BF16)
    pad_h = lambda a: jnp.pad(a, ((0, 0), (0, 0), (0, LANES - a.shape[2]))).reshape(a.shape[0], B_PAD)
    w_uq_p = pad_h(w_uq.reshape(Q_LORA, B_HEADS, B_QK)).astype(BF16)
    ukv = w_ukv.reshape(KV_LORA, B_HEADS, QK_NOPE + V_HEAD)
    wk_nope = pad_h(ukv[:, :, :QK_NOPE])
    eye = jnp.eye(QK_ROPE, dtype=F32)
    e_rope = jnp.pad(eye, ((0, 0), (QK_NOPE, LANES - B_QK)))
    wk_rope = jnp.tile(e_rope, (1, B_HEADS))
    w_k = jnp.concatenate([wk_nope, wk_rope, jnp.zeros((256 - KV_LORA - QK_ROPE, B_PAD), F32)], axis=0).astype(BF16)
    w_v = ukv[:, :, QK_NOPE:].reshape(KV_LORA, B_HEADS * V_HEAD).astype(BF16)
    pad_g = lambda g: jnp.tile(jnp.pad(g, (0, LANES - B_QK)), B_HEADS).reshape(1, B_PAD)
    idx = jnp.arange(256)
    bd = lambda n: (idx[:, None] // n == idx[None, :] // n).astype(BF16)
    ca, sua, sda = _rope_tables(seq, A_HEAD_DIM, 2, 0, 0)
    cb, sub, sdb = _rope_tables(seq, QK_ROPE, 1, QK_NOPE, LANES - B_QK)
    return {
        "w_ada": w_ada, "b_ada": b_ada,
        "g_attn": g_attn.reshape(1, D_MODEL),
        "w_in": w_in_p, "w_uq": w_uq_p, "w_k": w_k, "w_v": w_v,
        "g_cq": g_cq.reshape(1, Q_LORA), "g_ckv": g_ckv.reshape(1, KV_LORA),
        "gq_a": (jnp.tile(gq_a, A_HEADS) * (A_HEAD_DIM ** -0.5)).reshape(1, A_Q),
        "gk_a": jnp.tile(gk_a, A_KV_HEADS).reshape(1, A_KV),
        "gq_b": pad_g(gq_b) * (B_QK ** -0.5), "gk_b": pad_g(gk_b),
        "bd64": bd(A_HEAD_DIM), "bd128": bd(LANES),
        "ca": ca, "sua": sua, "sda": sda, "cb": cb, "sub": sub, "sdb": sdb,
        "sink": sink_a,
        "w_o_a": w_o_a.astype(BF16), "w_o_b": w_o_b.astype(BF16), "w_out": w_out.astype(BF16),
        "g_ffn": g_ffn.reshape(1, D_MODEL),
        "w_up": w_up.astype(BF16), "conv_w": conv_w, "conv_b": conv_b.reshape(1, 2 * D_FF),
        "w_down": w_down.astype(BF16),
    }


def _tile(s, pref):
    t = min(pref, s)
    assert s % t == 0 and t % WINDOW == 0, (s, t)
    return t


def _layer(x, c, w):
    b, s, _ = x.shape
    mod = _modulation(c, w["w_ada"], w["b_ada"]).reshape(b, 6, D_MODEL)
    qa, ka, va, qb, kb, vt, sga, sgb = _pre_attention(x, mod, w, _tile(s, 256))
    oa = _window_attention(qa, ka, va, w["sink"], _tile(s, 512))
    ob = _mla_attention(qb, kb, vt, _tile(s, 256))
    x1, h2 = _merge(x, mod, oa, ob, sga, sgb, w, _tile(s, 512))
    return _ffn(x1, mod, h2, w, _tile(s, 256))


def kernel(x_prompt, x_sample, c_prompt, c_sample, w_ada, b_ada, g_attn, w_in, gq_a, gk_a, sink_a, g_cq, w_uq,
           g_ckv, w_ukv, gq_b, gk_b, w_o_a, w_o_b, w_out, g_ffn, w_up, conv_w, conv_b, w_down):
    y_prompt, y_sample = x_prompt, x_sample
    for l in range(w_ada.shape[0]):
        p = (w_ada[l], b_ada[l], g_attn[l], w_in[l], gq_a[l], gk_a[l], sink_a[l], g_cq[l], w_uq[l], g_ckv[l],
             w_ukv[l], gq_b[l], gk_b[l], w_o_a[l], w_o_b[l], w_out[l], g_ffn[l], w_up[l], conv_w[l], conv_b[l],
             w_down[l])
        assert y_prompt.shape[1] == y_sample.shape[1]
        w = _prepare(y_prompt.shape[1], *p)
        y_prompt = _layer(y_prompt, c_prompt, w)
        y_sample = _layer(y_sample, c_sample, w)
    return (y_prompt, y_sample)
```

```python
import functools

import jax
import jax.numpy as jnp
from jax import lax
from jax.experimental import pallas as pl
from jax.experimental.pallas import tpu as pltpu

F32 = jnp.float32
BF16 = jnp.bfloat16

D_MODEL = 1024
A_HEADS = 8
A_KV_HEADS = 2
A_HEAD_DIM = 64
WINDOW = 128
B_HEADS = 8
Q_LORA = 256
KV_LORA = 128
QK_NOPE = 64
QK_ROPE = 32
V_HEAD = 64
D_FF = 2816
ROPE_THETA = 10000.0
EPS = 1e-6
NEG_INF = -1e30

A_Q = A_HEADS * A_HEAD_DIM
A_KV = A_KV_HEADS * A_HEAD_DIM
B_QK = QK_NOPE + QK_ROPE
LANES = 128
B_PAD = B_HEADS * LANES
VT_ROWS = 80

C_QA, C_KA, C_VA, C_CQ, C_CKV, C_KR, C_GA, C_GB, C_END = 0, 512, 640, 768, 1024, 1152, 1280, 2304, 3328

VMEM_LIMIT = 56 * 1024 * 1024
LOG2E = 1.4426950408889634
MAX_FIXED_SHIFT = 50.0
MLA_LOOKAHEAD = 4


def _cparams(n_axes):
    return pltpu.CompilerParams(dimension_semantics=("arbitrary",) * n_axes, vmem_limit_bytes=VMEM_LIMIT)


def _dot(a, b):
    return jnp.dot(a, b, preferred_element_type=F32)


def _dot_nt(a, b):
    return lax.dot_general(a, b, (((1,), (1,)), ((), ())), preferred_element_type=F32)


def _mod_kernel(c_ref, w_ref, b_ref, o_ref):
    c = c_ref[...]
    sc = c * jax.nn.sigmoid(c)
    o_ref[...] = _dot(sc.astype(BF16), w_ref[...].astype(BF16)) + b_ref[...]


def _modulation(c, w_ada, b_ada):
    b = c.shape[0]
    n = w_ada.shape[1]
    tn = 1024
    return pl.pallas_call(
        _mod_kernel,
        grid=(n // tn,),
        in_specs=[
            pl.BlockSpec((b, D_MODEL), lambda j: (0, 0)),
            pl.BlockSpec((D_MODEL, tn), lambda j: (0, j)),
            pl.BlockSpec((1, tn), lambda j: (0, j)),
        ],
        out_specs=pl.BlockSpec((b, tn), lambda j: (0, j)),
        out_shape=jax.ShapeDtypeStruct((b, n), F32),
        compiler_params=_cparams(1),
        name="adaln_mod",
    )(c, w_ada, b_ada.reshape(1, n))


def _group_norm(x, bd, inv_n, gain):
    ss = _dot((x * x).astype(BF16), bd)
    return x * lax.rsqrt(ss * inv_n + EPS) * gain


def _rope(x, c, s_up, s_dn, shift):
    return x * c + pltpu.roll(x, LANES - shift, 1) * s_up + pltpu.roll(x, shift, 1) * s_dn


def _pre_kernel(x_ref, mod_ref, g_ref, w_in_ref, w_uq_ref, w_k_ref, w_v_ref, gcq_ref, gckv_ref,
                gqa_ref, gka_ref, gqb_ref, gkb_ref, bd64_ref, bd128_ref,
                qbias_ref, kbias_ref, ca_ref, sua_ref, sda_ref, cb_ref, sub_ref, sdb_ref,
                qa_o, ka_o, va_o, qb_o, kb_o, vt_o, sga_o, sgb_o):
    x = x_ref[0]
    sh1 = mod_ref[0, 0:1, :]
    sc1 = mod_ref[0, 1:2, :]
    h = x * lax.rsqrt(jnp.mean(x * x, axis=-1, keepdims=True) + EPS) * g_ref[...]
    hb = (h * (1.0 + sc1) + sh1).astype(BF16)

    ca, sua, sda = ca_ref[...], sua_ref[...], sda_ref[...]
    cb, sub, sdb = cb_ref[...], sub_ref[...], sdb_ref[...]
    bd64 = bd64_ref[...]
    bd128 = bd128_ref[...]

    zq = _dot(hb, w_in_ref[:, C_QA:C_KA])
    for c in range(2):
        xn = _group_norm(zq[:, 256 * c:256 * (c + 1)], bd64, 1.0 / A_HEAD_DIM, gqa_ref[:, 256 * c:256 * (c + 1)])
        for j in range(2):
            lo = 256 * c + LANES * j
            qa_o[0, :, lo:lo + LANES] = _rope(xn[:, LANES * j:LANES * (j + 1)], ca, sua, sda, 32).astype(BF16)
    zk = _dot(hb, w_in_ref[:, C_KA:C_VA])
    ssk = _dot((zk * zk).astype(BF16), bd64[0:LANES, 0:LANES])
    kn = zk * lax.rsqrt(ssk * (1.0 / A_HEAD_DIM) + EPS) * gka_ref[...]
    ka_o[0] = _rope(kn, ca, sua, sda, 32).astype(BF16)
    va_o[0] = _dot(hb, w_in_ref[:, C_VA:C_CQ]).astype(BF16)

    cq = _dot(hb, w_in_ref[:, C_CQ:C_CKV])
    cqn = (cq * lax.rsqrt(jnp.mean(cq * cq, axis=-1, keepdims=True) + EPS) * gcq_ref[...]).astype(BF16)
    ckv = _dot(hb, w_in_ref[:, C_CKV:C_KR])
    ckvn = (ckv * lax.rsqrt(jnp.mean(ckv * ckv, axis=-1, keepdims=True) + EPS) * gckv_ref[...]).astype(BF16)
    krp = _dot(hb, w_in_ref[:, C_KR:C_GA]).astype(BF16)

    qraw = _dot(cqn, w_uq_ref[...])
    kraw = _dot(jnp.concatenate([ckvn, krp], axis=1), w_k_ref[...])
    for c in range(4):
        sl = slice(256 * c, 256 * (c + 1))
        qn = _group_norm(qraw[:, sl], bd128, 1.0 / B_QK, gqb_ref[:, sl])
        kn2 = _group_norm(kraw[:, sl], bd128, 1.0 / B_QK, gkb_ref[:, sl])
        for j in range(2):
            hh = 2 * c + j
            ls = slice(LANES * j, LANES * (j + 1))
            qb_o[0, hh] = (_rope(qn[:, ls], cb, sub, sdb, 16) + qbias_ref[...]).astype(BF16)
            kb_o[0, hh] = (_rope(kn2[:, ls], cb, sub, sdb, 16) + kbias_ref[...]).astype(BF16)

    vv = _dot(ckvn, w_v_ref[...])
    vt = vv.T
    t = vt.shape[1]
    row = lax.broadcasted_iota(jnp.int32, (VT_ROWS - V_HEAD, t), 0)
    tail = jnp.where(row == 0, 1.0, 0.0).astype(BF16)
    for hh in range(B_HEADS):
        vt_o[0, hh, 0, 0:V_HEAD, :] = vt[V_HEAD * hh:V_HEAD * (hh + 1), :].astype(BF16)
        vt_o[0, hh, 0, V_HEAD:VT_ROWS, :] = tail

    sga_o[0] = jax.nn.sigmoid(_dot(hb, w_in_ref[:, C_GA:C_GB])).astype(BF16)
    sgb_o[0] = jax.nn.sigmoid(_dot(hb, w_in_ref[:, C_GB:C_END])).astype(BF16)


def _pre_attention(x, mod, w, t):
    b, s, _ = x.shape
    nt = s // t
    const2 = lambda i, bb: (0, 0)
    tok2 = lambda i, bb: (i, 0)
    in_specs = [
        pl.BlockSpec((1, t, D_MODEL), lambda i, bb: (bb, i, 0)),
        pl.BlockSpec((1, 6, D_MODEL), lambda i, bb: (bb, 0, 0)),
        pl.BlockSpec((1, D_MODEL), const2),
        pl.BlockSpec((D_MODEL, C_END), const2),
        pl.BlockSpec((Q_LORA, B_PAD), const2),
        pl.BlockSpec((256, B_PAD), const2),
        pl.BlockSpec((KV_LORA, B_HEADS * V_HEAD), const2),
        pl.BlockSpec((1, Q_LORA), const2),
        pl.BlockSpec((1, KV_LORA), const2),
        pl.BlockSpec((1, A_Q), const2),
        pl.BlockSpec((1, A_KV), const2),
        pl.BlockSpec((1, B_PAD), const2),
        pl.BlockSpec((1, B_PAD), const2),
        pl.BlockSpec((256, 256), const2),
        pl.BlockSpec((256, 256), const2),
        pl.BlockSpec((1, LANES), const2),
        pl.BlockSpec((1, LANES), const2),
    ] + [pl.BlockSpec((t, LANES), tok2)] * 6
    out_shape = [
        jax.ShapeDtypeStruct((b, s, A_Q), BF16),
        jax.ShapeDtypeStruct((b, s, A_KV), BF16),
        jax.ShapeDtypeStruct((b, s, A_KV), BF16),
        jax.ShapeDtypeStruct((b, B_HEADS, s, LANES), BF16),
        jax.ShapeDtypeStruct((b, B_HEADS, s, LANES), BF16),
        jax.ShapeDtypeStruct((b, B_HEADS, nt, VT_ROWS, t), BF16),
        jax.ShapeDtypeStruct((b, s, D_MODEL), BF16),
        jax.ShapeDtypeStruct((b, s, D_MODEL), BF16),
    ]
    tok3 = lambda i, bb: (bb, i, 0)
    out_specs = [
        pl.BlockSpec((1, t, A_Q), tok3),
        pl.BlockSpec((1, t, A_KV), tok3),
        pl.BlockSpec((1, t, A_KV), tok3),
        pl.BlockSpec((1, B_HEADS, t, LANES), lambda i, bb: (bb, 0, i, 0)),
        pl.BlockSpec((1, B_HEADS, t, LANES), lambda i, bb: (bb, 0, i, 0)),
        pl.BlockSpec((1, B_HEADS, 1, VT_ROWS, t), lambda i, bb: (bb, 0, i, 0, 0)),
        pl.BlockSpec((1, t, D_MODEL), tok3),
        pl.BlockSpec((1, t, D_MODEL), tok3),
    ]
    return pl.pallas_call(
        _pre_kernel,
        grid=(nt, b),
        in_specs=in_specs,
        out_specs=out_specs,
        out_shape=out_shape,
        compiler_params=_cparams(2),
        name="pre_attention",
    )(x, mod, w["g_attn"], w["w_in"], w["w_uq"], w["w_k"], w["w_v"], w["g_cq"], w["g_ckv"],
      w["gq_a"], w["gk_a"], w["gq_b"], w["gk_b"], w["bd64"], w["bd128"], w["qbias"], w["kbias"],
      w["ca"], w["sua"], w["sda"], w["cb"], w["sub"], w["sdb"])


def _window_kernel(seq_len, tq, sink_ref, q_ref, kp_ref, km_ref, kn_ref, vp_ref, vm_ref, vn_ref, o_ref,
                   kcat, vcat):
    i = pl.program_id(1)
    blk = WINDOW
    kcat[0:blk] = kp_ref[0]
    kcat[blk:blk + tq] = km_ref[0]
    kcat[blk + tq:blk + tq + blk] = kn_ref[0]
    vcat[0:blk] = vp_ref[0]
    vcat[blk:blk + tq] = vm_ref[0]
    vcat[blk + tq:blk + tq + blk] = vn_ref[0]
    grp = A_HEADS // A_KV_HEADS
    for j in range(tq // blk):
        kb = kcat[j * blk:(j + 3) * blk]
        vb = vcat[j * blk:(j + 3) * blk]
        qpos = i * tq + j * blk + lax.broadcasted_iota(jnp.int32, (blk, 3 * blk), 0)
        kpos = i * tq + (j - 1) * blk + lax.broadcasted_iota(jnp.int32, (blk, 3 * blk), 1)
        valid = (jnp.abs(qpos - kpos) <= WINDOW) & (kpos >= 0) & (kpos < seq_len)
        for hk in range(A_KV_HEADS):
            kh = kb[:, hk * A_HEAD_DIM:(hk + 1) * A_HEAD_DIM]
            vh = vb[:, hk * A_HEAD_DIM:(hk + 1) * A_HEAD_DIM]
            for g in range(grp):
                hq = hk * grp + g
                qh = q_ref[0, j * blk:(j + 1) * blk, hq * A_HEAD_DIM:(hq + 1) * A_HEAD_DIM]
                sc = jnp.where(valid, _dot_nt(qh, kh), NEG_INF)
                sk = sink_ref[hq]
                m = jnp.maximum(jnp.max(sc, axis=-1, keepdims=True), sk)
                p = jnp.exp(sc - m)
                den = jnp.sum(p, axis=-1, keepdims=True) + jnp.exp(sk - m)
                o = _dot(p.astype(BF16), vh) / den
                o_ref[0, j * blk:(j + 1) * blk, hq * A_HEAD_DIM:(hq + 1) * A_HEAD_DIM] = o.astype(BF16)


def _window_attention(qa, ka, va, sink, tq):
    b, s, _ = qa.shape
    r = tq // WINDOW
    nblk = s // WINDOW
    main = lambda bb, i: (bb, i, 0)
    prev = lambda bb, i: (bb, jnp.maximum(i * r - 1, 0), 0)
    nxt = lambda bb, i: (bb, jnp.minimum((i + 1) * r, nblk - 1), 0)
    kv_specs = [
        pl.BlockSpec((1, WINDOW, A_KV), prev),
        pl.BlockSpec((1, tq, A_KV), main),
        pl.BlockSpec((1, WINDOW, A_KV), nxt),
    ]
    return pl.pallas_call(
        functools.partial(_window_kernel, s, tq),
        grid=(b, s // tq),
        in_specs=[pl.BlockSpec(memory_space=pltpu.SMEM), pl.BlockSpec((1, tq, A_Q), main)] + kv_specs + kv_specs,
        out_specs=pl.BlockSpec((1, tq, A_Q), main),
        out_shape=jax.ShapeDtypeStruct((b, s, A_Q), BF16),
        scratch_shapes=[pltpu.VMEM((tq + 2 * WINDOW, A_KV), BF16), pltpu.VMEM((tq + 2 * WINDOW, A_KV), BF16)],
        compiler_params=_cparams(2),
        name="window_attention",
    )(sink, qa, ka, ka, ka, va, va, va)


def _mla_safe_kernel(nc, q_ref, k_ref, vt_ref, o_ref, s_scr, o_scr):
    tk = vt_ref.shape[-1]
    tq = q_ref.shape[2]

    def head(h, carry):
        q = q_ref[0, h]

        def pass1(c, m):
            kc = k_ref[0, h, pl.ds(pl.multiple_of(c * tk, tk), tk), :]
            st = _dot_nt(kc, q)
            s_scr[c] = st
            return jnp.maximum(m, jnp.max(st, axis=0, keepdims=True))

        m = lax.fori_loop(0, nc, pass1, jnp.full((1, tq), -jnp.inf, F32))

        def pass2(c, acc):
            p = jnp.exp2(s_scr[c] - m).astype(BF16)
            return acc + _dot(vt_ref[0, h, c], p)

        acc = lax.fori_loop(0, nc, pass2, jnp.zeros((VT_ROWS, tq), F32))
        o_scr[pl.ds(pl.multiple_of(h * V_HEAD, V_HEAD), V_HEAD), :] = acc[0:V_HEAD] / acc[V_HEAD:V_HEAD + 1]
        return carry

    lax.fori_loop(0, B_HEADS, head, 0)
    o_ref[0] = o_scr[...].T.astype(BF16)


def _mla_fast_kernel(nc, q_ref, k_ref, vt_ref, o_ref, o_scr):
    tk = vt_ref.shape[-1]
    steps = [(h, c) for h in range(B_HEADS) for c in range(nc)]

    def scores(t):
        h, c = steps[t]
        return _dot_nt(k_ref[0, h, c * tk:(c + 1) * tk, :], q_ref[0, h])

    pending = {t: scores(t) for t in range(MLA_LOOKAHEAD)}
    acc = None
    for t, (h, c) in enumerate(steps):
        if t + MLA_LOOKAHEAD < len(steps):
            pending[t + MLA_LOOKAHEAD] = scores(t + MLA_LOOKAHEAD)
        p = jnp.exp2(pending.pop(t)).astype(BF16)
        d = _dot(vt_ref[0, h, c], p)
        acc = d if c == 0 else acc + d
        if c == nc - 1:
            o_scr[h * V_HEAD:(h + 1) * V_HEAD, :] = acc[0:V_HEAD] / acc[V_HEAD:V_HEAD + 1]
    o_ref[0] = o_scr[...].T.astype(BF16)


def _mla_attention(qb, kb, vt, tq, fast):
    b, _, s, _ = qb.shape
    nc, tk = vt.shape[2], vt.shape[4]
    scratch = [pltpu.VMEM((B_HEADS * V_HEAD, tq), F32)]
    if fast:
        body = functools.partial(_mla_fast_kernel, nc)
    else:
        body = functools.partial(_mla_safe_kernel, nc)
        scratch = [pltpu.VMEM((nc, tk, tq), F32)] + scratch
    return pl.pallas_call(
        body,
        grid=(b, s // tq),
        in_specs=[
            pl.BlockSpec((1, B_HEADS, tq, LANES), lambda bb, i: (bb, 0, i, 0)),
            pl.BlockSpec((1, B_HEADS, s, LANES), lambda bb, i: (bb, 0, 0, 0)),
            pl.BlockSpec((1, B_HEADS, nc, VT_ROWS, tk), lambda bb, i: (bb, 0, 0, 0, 0)),
        ],
        out_specs=pl.BlockSpec((1, tq, B_HEADS * V_HEAD), lambda bb, i: (bb, i, 0)),
        out_shape=jax.ShapeDtypeStruct((b, s, B_HEADS * V_HEAD), BF16),
        scratch_shapes=scratch,
        compiler_params=_cparams(2),
        name="mla_attention_fast" if fast else "mla_attention_safe",
    )(qb, kb, vt)


def _merge_kernel(x_ref, mod_ref, oa_ref, ob_ref, sga_ref, sgb_ref, woa_ref, wob_ref, wout_ref, g_ref,
                  x1_o, h2_o):
    gt1 = mod_ref[0, 2:3, :]
    sh2 = mod_ref[0, 3:4, :]
    sc2 = mod_ref[0, 4:5, :]
    mix = (sga_ref[0].astype(F32) * _dot(oa_ref[0], woa_ref[...])
           + sgb_ref[0].astype(F32) * _dot(ob_ref[0], wob_ref[...]))
    x1 = x_ref[0] + gt1 * _dot(mix.astype(BF16), wout_ref[...])
    x1_o[0] = x1
    h2 = x1 * lax.rsqrt(jnp.mean(x1 * x1, axis=-1, keepdims=True) + EPS) * g_ref[...]
    h2_o[0] = (h2 * (1.0 + sc2) + sh2).astype(BF16)


def _merge(x, mod, oa, ob, sga, sgb, w, t):
    b, s, _ = x.shape
    tok = lambda i, bb: (bb, i, 0)
    const2 = lambda i, bb: (0, 0)
    return pl.pallas_call(
        _merge_kernel,
        grid=(s // t, b),
        in_specs=[
            pl.BlockSpec((1, t, D_MODEL), tok),
            pl.BlockSpec((1, 6, D_MODEL), lambda i, bb: (bb, 0, 0)),
            pl.BlockSpec((1, t, A_Q), tok),
            pl.BlockSpec((1, t, B_HEADS * V_HEAD), tok),
            pl.BlockSpec((1, t, D_MODEL), tok),
            pl.BlockSpec((1, t, D_MODEL), tok),
            pl.BlockSpec((A_Q, D_MODEL), const2),
            pl.BlockSpec((B_HEADS * V_HEAD, D_MODEL), const2),
            pl.BlockSpec((D_MODEL, D_MODEL), const2),
            pl.BlockSpec((1, D_MODEL), const2),
        ],
        out_specs=[pl.BlockSpec((1, t, D_MODEL), tok), pl.BlockSpec((1, t, D_MODEL), tok)],
        out_shape=[jax.ShapeDtypeStruct((b, s, D_MODEL), F32), jax.ShapeDtypeStruct((b, s, D_MODEL), BF16)],
        compiler_params=_cparams(2),
        name="merge_outproj",
    )(x, mod, oa, ob, sga, sgb, w["w_o_a"], w["w_o_b"], w["w_out"], w["g_ffn"])


FF_CHUNK = D_FF // 2
HALO = 16


def _ffn_kernel(x1_ref, mod_ref, hp_ref, hm_ref, hn_ref, wup_ref, cw_ref, cb_ref, wdn_ref, y_o,
                hcat, ua_scr, ug_scr):
    i = pl.program_id(0)
    last = pl.num_programs(0) - 1
    t = hm_ref.shape[1]
    gt2 = mod_ref[0, 5:6, :]
    hcat[0:HALO] = jnp.where(i > 0, hp_ref[0], jnp.zeros_like(hp_ref[0]))
    hcat[HALO:HALO + t] = hm_ref[0]
    hcat[HALO + t:HALO + t + HALO] = jnp.where(i < last, hn_ref[0], jnp.zeros_like(hn_ref[0]))
    hc = hcat[...]
    acc = jnp.zeros((t, D_MODEL), F32)
    for c in range(D_FF // FF_CHUNK):
        branch = []
        for off, scr in ((0, ua_scr), (D_FF, ug_scr)):
            lo = off + c * FF_CHUNK
            scr[...] = _dot(hc, wup_ref[:, lo:lo + FF_CHUNK])
            u = cb_ref[:, lo:lo + FF_CHUNK]
            for j in range(3):
                u = u + scr[pl.ds(HALO - 1 + j, t), :] * cw_ref[j:j + 1, lo:lo + FF_CHUNK]
            branch.append(u)
        ua, ug = branch
        act = (ug * jax.nn.sigmoid(ug) * ua).astype(BF16)
        acc = acc + _dot(act, wdn_ref[c * FF_CHUNK:(c + 1) * FF_CHUNK, :])
    y_o[0] = x1_ref[0] + gt2 * acc


def _ffn(x1, mod, h2, w, t):
    b, s, _ = x1.shape
    r = t // HALO
    nh = s // HALO
    tok = lambda i, bb: (bb, i, 0)
    const2 = lambda i, bb: (0, 0)
    return pl.pallas_call(
        _ffn_kernel,
        grid=(s // t, b),
        in_specs=[
            pl.BlockSpec((1, t, D_MODEL), tok),
            pl.BlockSpec((1, 6, D_MODEL), lambda i, bb: (bb, 0, 0)),
            pl.BlockSpec((1, HALO, D_MODEL), lambda i, bb: (bb, jnp.maximum(i * r - 1, 0), 0)),
            pl.BlockSpec((1, t, D_MODEL), tok),
            pl.BlockSpec((1, HALO, D_MODEL), lambda i, bb: (bb, jnp.minimum((i + 1) * r, nh - 1), 0)),
            pl.BlockSpec((D_MODEL, 2 * D_FF), const2),
            pl.BlockSpec((3, 2 * D_FF), const2),
            pl.BlockSpec((1, 2 * D_FF), const2),
            pl.BlockSpec((D_FF, D_MODEL), const2),
        ],
        out_specs=pl.BlockSpec((1, t, D_MODEL), tok),
        out_shape=jax.ShapeDtypeStruct((b, s, D_MODEL), F32),
        scratch_shapes=[
            pltpu.VMEM((t + 2 * HALO, D_MODEL), BF16),
            pltpu.VMEM((t + 2 * HALO, FF_CHUNK), F32),
            pltpu.VMEM((t + 2 * HALO, FF_CHUNK), F32),
        ],
        compiler_params=_cparams(2),
        name="conv_ffn",
    )(x1, mod, h2, h2, h2, w["w_up"], w["conv_w"], w["conv_b"], w["w_down"])


def _rope_tables(seq, dim, n_rep, lead, pad):
    half = dim // 2
    inv = 1.0 / (ROPE_THETA ** (jnp.arange(0, dim, 2, dtype=F32) / dim))
    ang = jnp.arange(seq, dtype=F32)[:, None] * inv[None, :]
    cos, sin = jnp.cos(ang), jnp.sin(ang)
    zero = jnp.zeros_like(sin)
    c = jnp.concatenate([jnp.ones((seq, lead), F32)] + [cos, cos] * n_rep + [jnp.zeros((seq, pad), F32)], axis=1)
    s_up = jnp.concatenate([jnp.zeros((seq, lead), F32)] + [-sin, zero] * n_rep + [jnp.zeros((seq, pad), F32)], axis=1)
    s_dn = jnp.concatenate([jnp.zeros((seq, lead), F32)] + [zero, sin] * n_rep + [jnp.zeros((seq, pad), F32)], axis=1)
    del half
    return c, s_up, s_dn


def _prepare(seq, w_ada, b_ada, g_attn, w_in, gq_a, gk_a, sink_a, g_cq, w_uq, g_ckv, w_ukv, gq_b, gk_b,
             w_o_a, w_o_b, w_out, g_ffn, w_up, conv_w, conv_b, w_down):
    kr_lo = A_Q + 2 * A_KV + Q_LORA + KV_LORA
    kr_hi = kr_lo + QK_ROPE
    w_in_p = jnp.concatenate(
        [w_in[:, :kr_hi], jnp.zeros((D_MODEL, LANES - QK_ROPE), F32), w_in[:, kr_hi:]], axis=1).astype(BF16)
    pad_h = lambda a: jnp.pad(a, ((0, 0), (0, 0), (0, LANES - a.shape[2]))).reshape(a.shape[0], B_PAD)
    w_uq_p = pad_h(w_uq.reshape(Q_LORA, B_HEADS, B_QK)).astype(BF16)
    ukv = w_ukv.reshape(KV_LORA, B_HEADS, QK_NOPE + V_HEAD)
    wk_nope = pad_h(ukv[:, :, :QK_NOPE])
    eye = jnp.eye(QK_ROPE, dtype=F32)
    e_rope = jnp.pad(eye, ((0, 0), (QK_NOPE, LANES - B_QK)))
    wk_rope = jnp.tile(e_rope, (1, B_HEADS))
    w_k = jnp.concatenate([wk_nope, wk_rope, jnp.zeros((256 - KV_LORA - QK_ROPE, B_PAD), F32)], axis=0).astype(BF16)
    w_v = ukv[:, :, QK_NOPE:].reshape(KV_LORA, B_HEADS * V_HEAD).astype(BF16)
    pad_g = lambda g: jnp.tile(jnp.pad(g, (0, LANES - B_QK)), B_HEADS).reshape(1, B_PAD)
    idx = jnp.arange(256)
    bd = lambda n: (idx[:, None] // n == idx[None, :] // n).astype(BF16)
    ca, sua, sda = _rope_tables(seq, A_HEAD_DIM, 2, 0, 0)
    cb, sub, sdb = _rope_tables(seq, QK_ROPE, 1, QK_NOPE, LANES - B_QK)
    bound = (B_QK ** 0.5) * LOG2E * jnp.max(jnp.abs(gq_b)) * jnp.max(jnp.abs(gk_b))
    shift = jnp.ceil(bound * 1.02) + 1.0
    lane = jnp.arange(LANES)
    qbias = jnp.where(lane == B_QK, 1.0, 0.0).astype(F32).reshape(1, LANES)
    kbias = jnp.where(lane == B_QK, -shift, 0.0).astype(F32).reshape(1, LANES)
    return {
        "shift": shift, "qbias": qbias, "kbias": kbias,
        "w_ada": w_ada, "b_ada": b_ada,
        "g_attn": g_attn.reshape(1, D_MODEL),
        "w_in": w_in_p, "w_uq": w_uq_p, "w_k": w_k, "w_v": w_v,
        "g_cq": g_cq.reshape(1, Q_LORA), "g_ckv": g_ckv.reshape(1, KV_LORA),
        "gq_a": (jnp.tile(gq_a, A_HEADS) * (A_HEAD_DIM ** -0.5)).reshape(1, A_Q),
        "gk_a": jnp.tile(gk_a, A_KV_HEADS).reshape(1, A_KV),
        "gq_b": pad_g(gq_b) * (B_QK ** -0.5 * LOG2E), "gk_b": pad_g(gk_b),
        "bd64": bd(A_HEAD_DIM), "bd128": bd(LANES),
        "ca": ca, "sua": sua, "sda": sda, "cb": cb, "sub": sub, "sdb": sdb,
        "sink": sink_a,
        "w_o_a": w_o_a.astype(BF16), "w_o_b": w_o_b.astype(BF16), "w_out": w_out.astype(BF16),
        "g_ffn": g_ffn.reshape(1, D_MODEL),
        "w_up": w_up.astype(BF16), "conv_w": conv_w, "conv_b": conv_b.reshape(1, 2 * D_FF),
        "w_down": w_down.astype(BF16),
    }


def _tile(s, pref):
    t = min(pref, s)
    assert s % t == 0 and t % WINDOW == 0, (s, t)
    return t


def _layer(x, c, w):
    b, s, _ = x.shape
    mod = _modulation(c, w["w_ada"], w["b_ada"]).reshape(b, 6, D_MODEL)
    qa, ka, va, qb, kb, vt, sga, sgb = _pre_attention(x, mod, w, _tile(s, 256))
    oa = _window_attention(qa, ka, va, w["sink"], _tile(s, 512))
    tq = _tile(s, 256)
    ob = lax.cond(w["shift"] <= MAX_FIXED_SHIFT,
                  lambda *a: _mla_attention(*a, tq, True), lambda *a: _mla_attention(*a, tq, False), qb, kb, vt)
    x1, h2 = _merge(x, mod, oa, ob, sga, sgb, w, _tile(s, 512))
    return _ffn(x1, mod, h2, w, _tile(s, 256))


def kernel(x_prompt, x_sample, c_prompt, c_sample, w_ada, b_ada, g_attn, w_in, gq_a, gk_a, sink_a, g_cq, w_uq,
           g_ckv, w_ukv, gq_b, gk_b, w_o_a, w_o_b, w_out, g_ffn, w_up, conv_w, conv_b, w_down):
    y_prompt, y_sample = x_prompt, x_sample
    for l in range(w_ada.shape[0]):
        p = (w_ada[l], b_ada[l], g_attn[l], w_in[l], gq_a[l], gk_a[l], sink_a[l], g_cq[l], w_uq[l], g_ckv[l],
             w_ukv[l], gq_b[l], gk_b[l], w_o_a[l], w_o_b[l], w_out[l], g_ffn[l], w_up[l], conv_w[l], conv_b[l],
             w_down[l])
        assert y_prompt.shape[1] == y_sample.shape[1]
        w = _prepare(y_prompt.shape[1], *p)
        y_prompt = _layer(y_prompt, c_prompt, w)
        y_sample = _layer(y_sample, c_sample, w)
    return (y_prompt, y_sample)
```

```python
import functools

import jax
import jax.numpy as jnp
from jax import lax
from jax.experimental import pallas as pl
from jax.experimental.pallas import tpu as pltpu

F32 = jnp.float32
BF16 = jnp.bfloat16

D_MODEL = 1024
A_HEADS = 8
A_KV_HEADS = 2
A_HEAD_DIM = 64
WINDOW = 128
B_HEADS = 8
Q_LORA = 256
KV_LORA = 128
QK_NOPE = 64
QK_ROPE = 32
V_HEAD = 64
D_FF = 2816
ROPE_THETA = 10000.0
EPS = 1e-6
NEG_INF = -1e30

A_Q = A_HEADS * A_HEAD_DIM
A_KV = A_KV_HEADS * A_HEAD_DIM
B_QK = QK_NOPE + QK_ROPE
LANES = 128
B_PAD = B_HEADS * LANES
VT_ROWS = 80

C_QA, C_KA, C_VA, C_CQ, C_CKV, C_KR, C_GA, C_GB, C_END = 0, 512, 640, 768, 1024, 1152, 1280, 2304, 3328

VMEM_LIMIT = 56 * 1024 * 1024
LOG2E = 1.4426950408889634
MAX_FIXED_SHIFT = 50.0
MLA_LOOKAHEAD = 4


def _cparams(n_axes):
    return pltpu.CompilerParams(dimension_semantics=("arbitrary",) * n_axes, vmem_limit_bytes=VMEM_LIMIT)


def _dot(a, b):
    return jnp.dot(a, b, preferred_element_type=F32)


def _dot_nt(a, b):
    return lax.dot_general(a, b, (((1,), (1,)), ((), ())), preferred_element_type=F32)


def _mod_kernel(c_ref, w_ref, b_ref, o_ref):
    c = c_ref[...]
    sc = c * jax.nn.sigmoid(c)
    o_ref[...] = _dot(sc.astype(BF16), w_ref[...].astype(BF16)) + b_ref[...]


def _modulation(c, w_ada, b_ada):
    b = c.shape[0]
    n = w_ada.shape[1]
    tn = 1024
    return pl.pallas_call(
        _mod_kernel,
        grid=(n // tn,),
        in_specs=[
            pl.BlockSpec((b, D_MODEL), lambda j: (0, 0)),
            pl.BlockSpec((D_MODEL, tn), lambda j: (0, j)),
            pl.BlockSpec((1, tn), lambda j: (0, j)),
        ],
        out_specs=pl.BlockSpec((b, tn), lambda j: (0, j)),
        out_shape=jax.ShapeDtypeStruct((b, n), F32),
        compiler_params=_cparams(1),
        name="adaln_mod",
    )(c, w_ada, b_ada.reshape(1, n))


def _group_norm(x, bd, inv_n, gain):
    ss = _dot((x * x).astype(BF16), bd)
    return x * lax.rsqrt(ss * inv_n + EPS) * gain


def _rope(x, c, s_up, s_dn, shift):
    return x * c + pltpu.roll(x, LANES - shift, 1) * s_up + pltpu.roll(x, shift, 1) * s_dn


def _pre_kernel(x_ref, mod_ref, g_ref, w_in_ref, w_uq_ref, w_k_ref, w_v_ref, gcq_ref, gckv_ref,
                gqa_ref, gka_ref, gqb_ref, gkb_ref, bd64_ref, bd128_ref,
                qabias_ref, kabias_ref, qbias_ref, kbias_ref, ca_ref, sua_ref, sda_ref, cb_ref, sub_ref, sdb_ref,
                qa_o, ka_o, va_o, qb_o, kb_o, vt_o, sga_o, sgb_o):
    x = x_ref[0]
    sh1 = mod_ref[0, 0:1, :]
    sc1 = mod_ref[0, 1:2, :]
    h = x * lax.rsqrt(jnp.mean(x * x, axis=-1, keepdims=True) + EPS) * g_ref[...]
    hb = (h * (1.0 + sc1) + sh1).astype(BF16)

    ca, sua, sda = ca_ref[...], sua_ref[...], sda_ref[...]
    cb, sub, sdb = cb_ref[...], sub_ref[...], sdb_ref[...]
    bd64 = bd64_ref[...]
    bd128 = bd128_ref[...]

    t = x.shape[0]
    low_half = lax.broadcasted_iota(jnp.int32, (t, LANES), 1) < A_HEAD_DIM
    row = lax.broadcasted_iota(jnp.int32, (VT_ROWS - V_HEAD, LANES), 0)
    tail_a = jnp.where(row == 0, 1.0, 0.0).astype(BF16)

    def split_heads(pair):
        return jnp.where(low_half, pair, 0.0), jnp.where(low_half, pltpu.roll(pair, A_HEAD_DIM, 1), 0.0)

    zq = _dot(hb, w_in_ref[:, C_QA:C_KA])
    for c in range(2):
        xn = _group_norm(zq[:, 256 * c:256 * (c + 1)], bd64, 1.0 / A_HEAD_DIM, gqa_ref[:, 256 * c:256 * (c + 1)])
        for j in range(2):
            hq = 4 * c + 2 * j
            even, odd = split_heads(_rope(xn[:, LANES * j:LANES * (j + 1)], ca, sua, sda, 32))
            qa_o[0, hq] = (even + qabias_ref[hq]).astype(BF16)
            qa_o[0, hq + 1] = (odd + qabias_ref[hq + 1]).astype(BF16)
    zk = _dot(hb, w_in_ref[:, C_KA:C_VA])
    ssk = _dot((zk * zk).astype(BF16), bd64[0:LANES, 0:LANES])
    kn = zk * lax.rsqrt(ssk * (1.0 / A_HEAD_DIM) + EPS) * gka_ref[...]
    for hk, kh in enumerate(split_heads(_rope(kn, ca, sua, sda, 32))):
        ka_o[0, hk] = (kh + kabias_ref[...]).astype(BF16)
    vat = _dot(hb, w_in_ref[:, C_VA:C_CQ]).T
    for hk in range(A_KV_HEADS):
        for blk in range(t // WINDOW):
            va_o[0, hk, blk, 0:A_HEAD_DIM, :] = vat[A_HEAD_DIM * hk:A_HEAD_DIM * (hk + 1),
                                                    WINDOW * blk:WINDOW * (blk + 1)].astype(BF16)
            va_o[0, hk, blk, A_HEAD_DIM:VT_ROWS, :] = tail_a

    cq = _dot(hb, w_in_ref[:, C_CQ:C_CKV])
    cqn = (cq * lax.rsqrt(jnp.mean(cq * cq, axis=-1, keepdims=True) + EPS) * gcq_ref[...]).astype(BF16)
    ckv = _dot(hb, w_in_ref[:, C_CKV:C_KR])
    ckvn = (ckv * lax.rsqrt(jnp.mean(ckv * ckv, axis=-1, keepdims=True) + EPS) * gckv_ref[...]).astype(BF16)
    krp = _dot(hb, w_in_ref[:, C_KR:C_GA]).astype(BF16)

    qraw = _dot(cqn, w_uq_ref[...])
    kraw = _dot(jnp.concatenate([ckvn, krp], axis=1), w_k_ref[...])
    for c in range(4):
        sl = slice(256 * c, 256 * (c + 1))
        qn = _group_norm(qraw[:, sl], bd128, 1.0 / B_QK, gqb_ref[:, sl])
        kn2 = _group_norm(kraw[:, sl], bd128, 1.0 / B_QK, gkb_ref[:, sl])
        for j in range(2):
            hh = 2 * c + j
            ls = slice(LANES * j, LANES * (j + 1))
            qb_o[0, hh] = (_rope(qn[:, ls], cb, sub, sdb, 16) + qbias_ref[...]).astype(BF16)
            kb_o[0, hh] = (_rope(kn2[:, ls], cb, sub, sdb, 16) + kbias_ref[...]).astype(BF16)

    vv = _dot(ckvn, w_v_ref[...])
    vt = vv.T
    row = lax.broadcasted_iota(jnp.int32, (VT_ROWS - V_HEAD, t), 0)
    tail = jnp.where(row == 0, 1.0, 0.0).astype(BF16)
    for hh in range(B_HEADS):
        vt_o[0, hh, 0, 0:V_HEAD, :] = vt[V_HEAD * hh:V_HEAD * (hh + 1), :].astype(BF16)
        vt_o[0, hh, 0, V_HEAD:VT_ROWS, :] = tail

    sga_o[0] = jax.nn.sigmoid(_dot(hb, w_in_ref[:, C_GA:C_GB])).astype(BF16)
    sgb_o[0] = jax.nn.sigmoid(_dot(hb, w_in_ref[:, C_GB:C_END])).astype(BF16)


def _pre_attention(x, mod, w, t):
    b, s, _ = x.shape
    nt = s // t
    const2 = lambda i, bb: (0, 0)
    tok2 = lambda i, bb: (i, 0)
    in_specs = [
        pl.BlockSpec((1, t, D_MODEL), lambda i, bb: (bb, i, 0)),
        pl.BlockSpec((1, 6, D_MODEL), lambda i, bb: (bb, 0, 0)),
        pl.BlockSpec((1, D_MODEL), const2),
        pl.BlockSpec((D_MODEL, C_END), const2),
        pl.BlockSpec((Q_LORA, B_PAD), const2),
        pl.BlockSpec((256, B_PAD), const2),
        pl.BlockSpec((KV_LORA, B_HEADS * V_HEAD), const2),
        pl.BlockSpec((1, Q_LORA), const2),
        pl.BlockSpec((1, KV_LORA), const2),
        pl.BlockSpec((1, A_Q), const2),
        pl.BlockSpec((1, A_KV), const2),
        pl.BlockSpec((1, B_PAD), const2),
        pl.BlockSpec((1, B_PAD), const2),
        pl.BlockSpec((256, 256), const2),
        pl.BlockSpec((256, 256), const2),
        pl.BlockSpec((A_HEADS, 1, LANES), lambda i, bb: (0, 0, 0)),
        pl.BlockSpec((1, LANES), const2),
        pl.BlockSpec((1, LANES), const2),
        pl.BlockSpec((1, LANES), const2),
    ] + [pl.BlockSpec((t, LANES), tok2)] * 6
    out_shape = [
        jax.ShapeDtypeStruct((b, A_HEADS, s, LANES), BF16),
        jax.ShapeDtypeStruct((b, A_KV_HEADS, s, LANES), BF16),
        jax.ShapeDtypeStruct((b, A_KV_HEADS, s // WINDOW, VT_ROWS, WINDOW), BF16),
        jax.ShapeDtypeStruct((b, B_HEADS, s, LANES), BF16),
        jax.ShapeDtypeStruct((b, B_HEADS, s, LANES), BF16),
        jax.ShapeDtypeStruct((b, B_HEADS, nt, VT_ROWS, t), BF16),
        jax.ShapeDtypeStruct((b, s, D_MODEL), BF16),
        jax.ShapeDtypeStruct((b, s, D_MODEL), BF16),
    ]
    tok3 = lambda i, bb: (bb, i, 0)
    out_specs = [
        pl.BlockSpec((1, A_HEADS, t, LANES), lambda i, bb: (bb, 0, i, 0)),
        pl.BlockSpec((1, A_KV_HEADS, t, LANES), lambda i, bb: (bb, 0, i, 0)),
        pl.BlockSpec((1, A_KV_HEADS, t // WINDOW, VT_ROWS, WINDOW), lambda i, bb: (bb, 0, i, 0, 0)),
        pl.BlockSpec((1, B_HEADS, t, LANES), lambda i, bb: (bb, 0, i, 0)),
        pl.BlockSpec((1, B_HEADS, t, LANES), lambda i, bb: (bb, 0, i, 0)),
        pl.BlockSpec((1, B_HEADS, 1, VT_ROWS, t), lambda i, bb: (bb, 0, i, 0, 0)),
        pl.BlockSpec((1, t, D_MODEL), tok3),
        pl.BlockSpec((1, t, D_MODEL), tok3),
    ]
    return pl.pallas_call(
        _pre_kernel,
        grid=(nt, b),
        in_specs=in_specs,
        out_specs=out_specs,
        out_shape=out_shape,
        compiler_params=_cparams(2),
        name="pre_attention",
    )(x, mod, w["g_attn"], w["w_in"], w["w_uq"], w["w_k"], w["w_v"], w["g_cq"], w["g_ckv"],
      w["gq_a"], w["gk_a"], w["gq_b"], w["gk_b"], w["bd64"], w["bd128"], w["qabias"], w["kabias"],
      w["qbias"], w["kbias"],
      w["ca"], w["sua"], w["sda"], w["cb"], w["sub"], w["sdb"])


def _window_kernel(r, fast, sink_ref, q_ref, kp_ref, km_ref, kn_ref, vp_ref, vm_ref, vn_ref, o_ref,
                   kcat, bias_scr, o_scr):
    i = pl.program_id(1)
    last_i = pl.num_programs(1) - 1
    blk = WINDOW
    grp = A_HEADS // A_KV_HEADS
    nq = grp * blk

    @pl.when(i == 0)
    def _():
        kk = lax.broadcasted_iota(jnp.int32, (3 * blk, nq), 0)
        qq = lax.broadcasted_iota(jnp.int32, (3 * blk, nq), 1) & (blk - 1)
        base = jnp.where(kk >= qq, jnp.where(kk <= qq + 2 * blk, 0.0, NEG_INF), NEG_INF)
        bias_scr[0] = base
        bias_scr[1] = jnp.where(kk >= blk, base, NEG_INF)
        bias_scr[2] = jnp.where(kk < 2 * blk, base, NEG_INF)

    for hk in range(A_KV_HEADS):
        kcat[hk, 0:blk] = kp_ref[0, hk]
        kcat[hk, blk:(r + 1) * blk] = km_ref[0, hk]
        kcat[hk, (r + 1) * blk:(r + 2) * blk] = kn_ref[0, hk]
    units = [(j, hk) for j in range(r) for hk in range(A_KV_HEADS)]

    def scores(u):
        j, hk = units[u]
        q4 = jnp.concatenate([q_ref[0, hk * grp + g, j * blk:(j + 1) * blk, :] for g in range(grp)], axis=0)
        return _dot_nt(kcat[hk, j * blk:(j + 3) * blk, :], q4)

    pending = {0: scores(0)}
    for u, (j, hk) in enumerate(units):
        if u + 1 < len(units):
            pending[u + 1] = scores(u + 1)
        if j == 0:
            bias = bias_scr[jnp.where(i == 0, 1, 0)]
        elif j == r - 1:
            bias = bias_scr[jnp.where(i == last_i, 2, 0)]
        else:
            bias = bias_scr[0]
        st = pending.pop(u) + bias
        if fast:
            p = jnp.exp2(st).astype(BF16)
            sink_term = sink_ref[hk]
        else:
            srow = sink_ref[hk]
            m = jnp.maximum(jnp.max(st, axis=0, keepdims=True), srow)
            p = jnp.exp2(st - m).astype(BF16)
            sink_term = jnp.exp2(srow - m)
        vblk = [vp_ref[0, hk, 0]] + [vm_ref[0, hk, n] for n in range(r)] + [vn_ref[0, hk, 0]]
        acc = _dot(jnp.concatenate(vblk[j:j + 3], axis=1), p)
        ot = acc[0:A_HEAD_DIM] / (acc[A_HEAD_DIM:A_HEAD_DIM + 1] + sink_term)
        for g in range(grp):
            hq = hk * grp + g
            o_scr[hq * A_HEAD_DIM:(hq + 1) * A_HEAD_DIM, j * blk:(j + 1) * blk] = ot[:, g * blk:(g + 1) * blk]
    o_ref[0] = o_scr[...].T.astype(BF16)


def _window_attention(qa, ka, vat, sink_rows, tq, fast):
    b, _, s, _ = qa.shape
    r = tq // WINDOW
    nblk = s // WINDOW
    assert r >= 2
    grp = A_HEADS // A_KV_HEADS
    prev = lambda i: jnp.maximum(i * r - 1, 0)
    nxt = lambda i: jnp.minimum((i + 1) * r, nblk - 1)
    k_specs = [
        pl.BlockSpec((1, A_KV_HEADS, WINDOW, LANES), lambda bb, i: (bb, 0, prev(i), 0)),
        pl.BlockSpec((1, A_KV_HEADS, tq, LANES), lambda bb, i: (bb, 0, i, 0)),
        pl.BlockSpec((1, A_KV_HEADS, WINDOW, LANES), lambda bb, i: (bb, 0, nxt(i), 0)),
    ]
    v_specs = [
        pl.BlockSpec((1, A_KV_HEADS, 1, VT_ROWS, WINDOW), lambda bb, i: (bb, 0, prev(i), 0, 0)),
        pl.BlockSpec((1, A_KV_HEADS, r, VT_ROWS, WINDOW), lambda bb, i: (bb, 0, i, 0, 0)),
        pl.BlockSpec((1, A_KV_HEADS, 1, VT_ROWS, WINDOW), lambda bb, i: (bb, 0, nxt(i), 0, 0)),
    ]
    return pl.pallas_call(
        functools.partial(_window_kernel, r, fast),
        grid=(b, s // tq),
        in_specs=[pl.BlockSpec((A_KV_HEADS, 1, grp * WINDOW), lambda bb, i: (0, 0, 0)),
                  pl.BlockSpec((1, A_HEADS, tq, LANES), lambda bb, i: (bb, 0, i, 0))] + k_specs + v_specs,
        out_specs=pl.BlockSpec((1, tq, A_Q), lambda bb, i: (bb, i, 0)),
        out_shape=jax.ShapeDtypeStruct((b, s, A_Q), BF16),
        scratch_shapes=[pltpu.VMEM((A_KV_HEADS, tq + 2 * WINDOW, LANES), BF16),
                        pltpu.VMEM((3, 3 * WINDOW, grp * WINDOW), F32),
                        pltpu.VMEM((A_Q, tq), F32)],
        compiler_params=_cparams(2),
        name="window_attention_fast" if fast else "window_attention_safe",
    )(sink_rows, qa, ka, ka, ka, vat, vat, vat)


def _mla_safe_kernel(nc, q_ref, k_ref, vt_ref, o_ref, s_scr, o_scr):
    tk = vt_ref.shape[-1]
    tq = q_ref.shape[2]

    def head(h, carry):
        q = q_ref[0, h]

        def pass1(c, m):
            kc = k_ref[0, h, pl.ds(pl.multiple_of(c * tk, tk), tk), :]
            st = _dot_nt(kc, q)
            s_scr[c] = st
            return jnp.maximum(m, jnp.max(st, axis=0, keepdims=True))

        m = lax.fori_loop(0, nc, pass1, jnp.full((1, tq), -jnp.inf, F32))

        def pass2(c, acc):
            p = jnp.exp2(s_scr[c] - m).astype(BF16)
            return acc + _dot(vt_ref[0, h, c], p)

        acc = lax.fori_loop(0, nc, pass2, jnp.zeros((VT_ROWS, tq), F32))
        o_scr[pl.ds(pl.multiple_of(h * V_HEAD, V_HEAD), V_HEAD), :] = acc[0:V_HEAD] / acc[V_HEAD:V_HEAD + 1]
        return carry

    lax.fori_loop(0, B_HEADS, head, 0)
    o_ref[0] = o_scr[...].T.astype(BF16)


def _mla_fast_kernel(nc, q_ref, k_ref, vt_ref, o_ref, o_scr):
    tk = vt_ref.shape[-1]
    steps = [(h, c) for h in range(B_HEADS) for c in range(nc)]

    def scores(t):
        h, c = steps[t]
        return _dot_nt(k_ref[0, h, c * tk:(c + 1) * tk, :], q_ref[0, h])

    pending = {t: scores(t) for t in range(MLA_LOOKAHEAD)}
    acc = None
    for t, (h, c) in enumerate(steps):
        if t + MLA_LOOKAHEAD < len(steps):
            pending[t + MLA_LOOKAHEAD] = scores(t + MLA_LOOKAHEAD)
        p = jnp.exp2(pending.pop(t)).astype(BF16)
        d = _dot(vt_ref[0, h, c], p)
        acc = d if c == 0 else acc + d
        if c == nc - 1:
            o_scr[h * V_HEAD:(h + 1) * V_HEAD, :] = acc[0:V_HEAD] / acc[V_HEAD:V_HEAD + 1]
    o_ref[0] = o_scr[...].T.astype(BF16)


def _mla_attention(qb, kb, vt, tq, fast):
    b, _, s, _ = qb.shape
    nc, tk = vt.shape[2], vt.shape[4]
    scratch = [pltpu.VMEM((B_HEADS * V_HEAD, tq), F32)]
    if fast:
        body = functools.partial(_mla_fast_kernel, nc)
    else:
        body = functools.partial(_mla_safe_kernel, nc)
        scratch = [pltpu.VMEM((nc, tk, tq), F32)] + scratch
    return pl.pallas_call(
        body,
        grid=(b, s // tq),
        in_specs=[
            pl.BlockSpec((1, B_HEADS, tq, LANES), lambda bb, i: (bb, 0, i, 0)),
            pl.BlockSpec((1, B_HEADS, s, LANES), lambda bb, i: (bb, 0, 0, 0)),
            pl.BlockSpec((1, B_HEADS, nc, VT_ROWS, tk), lambda bb, i: (bb, 0, 0, 0, 0)),
        ],
        out_specs=pl.BlockSpec((1, tq, B_HEADS * V_HEAD), lambda bb, i: (bb, i, 0)),
        out_shape=jax.ShapeDtypeStruct((b, s, B_HEADS * V_HEAD), BF16),
        scratch_shapes=scratch,
        compiler_params=_cparams(2),
        name="mla_attention_fast" if fast else "mla_attention_safe",
    )(qb, kb, vt)


def _merge_kernel(x_ref, mod_ref, oa_ref, ob_ref, sga_ref, sgb_ref, woa_ref, wob_ref, wout_ref, g_ref,
                  x1_o, h2_o):
    gt1 = mod_ref[0, 2:3, :]
    sh2 = mod_ref[0, 3:4, :]
    sc2 = mod_ref[0, 4:5, :]
    mix = (sga_ref[0].astype(F32) * _dot(oa_ref[0], woa_ref[...])
           + sgb_ref[0].astype(F32) * _dot(ob_ref[0], wob_ref[...]))
    x1 = x_ref[0] + gt1 * _dot(mix.astype(BF16), wout_ref[...])
    x1_o[0] = x1
    h2 = x1 * lax.rsqrt(jnp.mean(x1 * x1, axis=-1, keepdims=True) + EPS) * g_ref[...]
    h2_o[0] = (h2 * (1.0 + sc2) + sh2).astype(BF16)


def _merge(x, mod, oa, ob, sga, sgb, w, t):
    b, s, _ = x.shape
    tok = lambda i, bb: (bb, i, 0)
    const2 = lambda i, bb: (0, 0)
    return pl.pallas_call(
        _merge_kernel,
        grid=(s // t, b),
        in_specs=[
            pl.BlockSpec((1, t, D_MODEL), tok),
            pl.BlockSpec((1, 6, D_MODEL), lambda i, bb: (bb, 0, 0)),
            pl.BlockSpec((1, t, A_Q), tok),
            pl.BlockSpec((1, t, B_HEADS * V_HEAD), tok),
            pl.BlockSpec((1, t, D_MODEL), tok),
            pl.BlockSpec((1, t, D_MODEL), tok),
            pl.BlockSpec((A_Q, D_MODEL), const2),
            pl.BlockSpec((B_HEADS * V_HEAD, D_MODEL), const2),
            pl.BlockSpec((D_MODEL, D_MODEL), const2),
            pl.BlockSpec((1, D_MODEL), const2),
        ],
        out_specs=[pl.BlockSpec((1, t, D_MODEL), tok), pl.BlockSpec((1, t, D_MODEL), tok)],
        out_shape=[jax.ShapeDtypeStruct((b, s, D_MODEL), F32), jax.ShapeDtypeStruct((b, s, D_MODEL), BF16)],
        compiler_params=_cparams(2),
        name="merge_outproj",
    )(x, mod, oa, ob, sga, sgb, w["w_o_a"], w["w_o_b"], w["w_out"], w["g_ffn"])


FF_CHUNK = D_FF // 2
HALO = 16


def _ffn_kernel(x1_ref, mod_ref, hp_ref, hm_ref, hn_ref, wup_ref, cw_ref, cb_ref, wdn_ref, y_o,
                hcat, ua_scr, ug_scr):
    i = pl.program_id(0)
    last = pl.num_programs(0) - 1
    t = hm_ref.shape[1]
    gt2 = mod_ref[0, 5:6, :]
    hcat[0:HALO] = jnp.where(i > 0, hp_ref[0], jnp.zeros_like(hp_ref[0]))
    hcat[HALO:HALO + t] = hm_ref[0]
    hcat[HALO + t:HALO + t + HALO] = jnp.where(i < last, hn_ref[0], jnp.zeros_like(hn_ref[0]))
    hc = hcat[...]
    acc = jnp.zeros((t, D_MODEL), F32)
    for c in range(D_FF // FF_CHUNK):
        branch = []
        for off, scr in ((0, ua_scr), (D_FF, ug_scr)):
            lo = off + c * FF_CHUNK
            scr[...] = _dot(hc, wup_ref[:, lo:lo + FF_CHUNK])
            u = cb_ref[:, lo:lo + FF_CHUNK]
            for j in range(3):
                u = u + scr[pl.ds(HALO - 1 + j, t), :] * cw_ref[j:j + 1, lo:lo + FF_CHUNK]
            branch.append(u)
        ua, ug = branch
        act = (ug * jax.nn.sigmoid(ug) * ua).astype(BF16)
        acc = acc + _dot(act, wdn_ref[c * FF_CHUNK:(c + 1) * FF_CHUNK, :])
    y_o[0] = x1_ref[0] + gt2 * acc


def _ffn(x1, mod, h2, w, t):
    b, s, _ = x1.shape
    r = t // HALO
    nh = s // HALO
    tok = lambda i, bb: (bb, i, 0)
    const2 = lambda i, bb: (0, 0)
    return pl.pallas_call(
        _ffn_kernel,
        grid=(s // t, b),
        in_specs=[
            pl.BlockSpec((1, t, D_MODEL), tok),
            pl.BlockSpec((1, 6, D_MODEL), lambda i, bb: (bb, 0, 0)),
            pl.BlockSpec((1, HALO, D_MODEL), lambda i, bb: (bb, jnp.maximum(i * r - 1, 0), 0)),
            pl.BlockSpec((1, t, D_MODEL), tok),
            pl.BlockSpec((1, HALO, D_MODEL), lambda i, bb: (bb, jnp.minimum((i + 1) * r, nh - 1), 0)),
            pl.BlockSpec((D_MODEL, 2 * D_FF), const2),
            pl.BlockSpec((3, 2 * D_FF), const2),
            pl.BlockSpec((1, 2 * D_FF), const2),
            pl.BlockSpec((D_FF, D_MODEL), const2),
        ],
        out_specs=pl.BlockSpec((1, t, D_MODEL), tok),
        out_shape=jax.ShapeDtypeStruct((b, s, D_MODEL), F32),
        scratch_shapes=[
            pltpu.VMEM((t + 2 * HALO, D_MODEL), BF16),
            pltpu.VMEM((t + 2 * HALO, FF_CHUNK), F32),
            pltpu.VMEM((t + 2 * HALO, FF_CHUNK), F32),
        ],
        compiler_params=_cparams(2),
        name="conv_ffn",
    )(x1, mod, h2, h2, h2, w["w_up"], w["conv_w"], w["conv_b"], w["w_down"])


def _rope_tables(seq, dim, n_rep, lead, pad):
    half = dim // 2
    inv = 1.0 / (ROPE_THETA ** (jnp.arange(0, dim, 2, dtype=F32) / dim))
    ang = jnp.arange(seq, dtype=F32)[:, None] * inv[None, :]
    cos, sin = jnp.cos(ang), jnp.sin(ang)
    zero = jnp.zeros_like(sin)
    c = jnp.concatenate([jnp.ones((seq, lead), F32)] + [cos, cos] * n_rep + [jnp.zeros((seq, pad), F32)], axis=1)
    s_up = jnp.concatenate([jnp.zeros((seq, lead), F32)] + [-sin, zero] * n_rep + [jnp.zeros((seq, pad), F32)], axis=1)
    s_dn = jnp.concatenate([jnp.zeros((seq, lead), F32)] + [zero, sin] * n_rep + [jnp.zeros((seq, pad), F32)], axis=1)
    del half
    return c, s_up, s_dn


def _prepare(seq, w_ada, b_ada, g_attn, w_in, gq_a, gk_a, sink_a, g_cq, w_uq, g_ckv, w_ukv, gq_b, gk_b,
             w_o_a, w_o_b, w_out, g_ffn, w_up, conv_w, conv_b, w_down):
    kr_lo = A_Q + 2 * A_KV + Q_LORA + KV_LORA
    kr_hi = kr_lo + QK_ROPE
    w_in_p = jnp.concatenate(
        [w_in[:, :kr_hi], jnp.zeros((D_MODEL, LANES - QK_ROPE), F32), w_in[:, kr_hi:]], axis=1).astype(BF16)
    pad_h = lambda a: jnp.pad(a, ((0, 0), (0, 0), (0, LANES - a.shape[2]))).reshape(a.shape[0], B_PAD)
    w_uq_p = pad_h(w_uq.reshape(Q_LORA, B_HEADS, B_QK)).astype(BF16)
    ukv = w_ukv.reshape(KV_LORA, B_HEADS, QK_NOPE + V_HEAD)
    wk_nope = pad_h(ukv[:, :, :QK_NOPE])
    eye = jnp.eye(QK_ROPE, dtype=F32)
    e_rope = jnp.pad(eye, ((0, 0), (QK_NOPE, LANES - B_QK)))
    wk_rope = jnp.tile(e_rope, (1, B_HEADS))
    w_k = jnp.concatenate([wk_nope, wk_rope, jnp.zeros((256 - KV_LORA - QK_ROPE, B_PAD), F32)], axis=0).astype(BF16)
    w_v = ukv[:, :, QK_NOPE:].reshape(KV_LORA, B_HEADS * V_HEAD).astype(BF16)
    pad_g = lambda g: jnp.tile(jnp.pad(g, (0, LANES - B_QK)), B_HEADS).reshape(1, B_PAD)
    idx = jnp.arange(256)
    bd = lambda n: (idx[:, None] // n == idx[None, :] // n).astype(BF16)
    ca, sua, sda = _rope_tables(seq, A_HEAD_DIM, 2, 0, 0)
    cb, sub, sdb = _rope_tables(seq, QK_ROPE, 1, QK_NOPE, LANES - B_QK)
    bound = (B_QK ** 0.5) * LOG2E * jnp.max(jnp.abs(gq_b)) * jnp.max(jnp.abs(gk_b))
    shift = jnp.ceil(bound * 1.02) + 1.0
    lane = jnp.arange(LANES)
    qbias = jnp.where(lane == B_QK, 1.0, 0.0).astype(F32).reshape(1, LANES)
    kbias = jnp.where(lane == B_QK, -shift, 0.0).astype(F32).reshape(1, LANES)
    bound_a = (A_HEAD_DIM ** 0.5) * LOG2E * jnp.max(jnp.abs(gq_a)) * jnp.max(jnp.abs(gk_a))
    shift_a0 = jnp.ceil(bound_a * 1.02) + 1.0
    sink2 = sink_a * LOG2E
    shift_a = jnp.maximum(shift_a0, sink2).astype(BF16).astype(F32)
    qabias = jnp.where(lane[None, :] == A_HEAD_DIM, -shift_a[:, None], 0.0).reshape(A_HEADS, 1, LANES)
    kabias = jnp.where(lane == A_HEAD_DIM, 1.0, 0.0).astype(F32).reshape(1, LANES)
    rows = lambda v: jnp.repeat(v, WINDOW).reshape(A_KV_HEADS, 1, A_HEADS // A_KV_HEADS * WINDOW)
    return {
        "shift": shift, "qbias": qbias, "kbias": kbias,
        "shift_a": shift_a0, "qabias": qabias, "kabias": kabias,
        "sink_fast": rows(jnp.exp2(sink2 - shift_a)), "sink_safe": rows(sink2 - shift_a),
        "w_ada": w_ada, "b_ada": b_ada,
        "g_attn": g_attn.reshape(1, D_MODEL),
        "w_in": w_in_p, "w_uq": w_uq_p, "w_k": w_k, "w_v": w_v,
        "g_cq": g_cq.reshape(1, Q_LORA), "g_ckv": g_ckv.reshape(1, KV_LORA),
        "gq_a": (jnp.tile(gq_a, A_HEADS) * (A_HEAD_DIM ** -0.5 * LOG2E)).reshape(1, A_Q),
        "gk_a": jnp.tile(gk_a, A_KV_HEADS).reshape(1, A_KV),
        "gq_b": pad_g(gq_b) * (B_QK ** -0.5 * LOG2E), "gk_b": pad_g(gk_b),
        "bd64": bd(A_HEAD_DIM), "bd128": bd(LANES),
        "ca": ca, "sua": sua, "sda": sda, "cb": cb, "sub": sub, "sdb": sdb,
        "w_o_a": w_o_a.astype(BF16), "w_o_b": w_o_b.astype(BF16), "w_out": w_out.astype(BF16),
        "g_ffn": g_ffn.reshape(1, D_MODEL),
        "w_up": w_up.astype(BF16), "conv_w": conv_w, "conv_b": conv_b.reshape(1, 2 * D_FF),
        "w_down": w_down.astype(BF16),
    }


def _tile(s, pref):
    t = min(pref, s)
    assert s % t == 0 and t % WINDOW == 0, (s, t)
    return t


def _layer(x, c, w):
    b, s, _ = x.shape
    mod = _modulation(c, w["w_ada"], w["b_ada"]).reshape(b, 6, D_MODEL)
    qa, ka, va, qb, kb, vt, sga, sgb = _pre_attention(x, mod, w, _tile(s, 256))
    tw = _tile(s, 512)
    oa = lax.cond(w["shift_a"] <= MAX_FIXED_SHIFT,
                  lambda *a: _window_attention(*a, w["sink_fast"], tw, True),
                  lambda *a: _window_attention(*a, w["sink_safe"], tw, False), qa, ka, va)
    tq = _tile(s, 256)
    ob = lax.cond(w["shift"] <= MAX_FIXED_SHIFT,
                  lambda *a: _mla_attention(*a, tq, True), lambda *a: _mla_attention(*a, tq, False), qb, kb, vt)
    x1, h2 = _merge(x, mod, oa, ob, sga, sgb, w, _tile(s, 512))
    return _ffn(x1, mod, h2, w, _tile(s, 256))


def kernel(x_prompt, x_sample, c_prompt, c_sample, w_ada, b_ada, g_attn, w_in, gq_a, gk_a, sink_a, g_cq, w_uq,
           g_ckv, w_ukv, gq_b, gk_b, w_o_a, w_o_b, w_out, g_ffn, w_up, conv_w, conv_b, w_down):
    y_prompt, y_sample = x_prompt, x_sample
    for l in range(w_ada.shape[0]):
        p = (w_ada[l], b_ada[l], g_attn[l], w_in[l], gq_a[l], gk_a[l], sink_a[l], g_cq[l], w_uq[l], g_ckv[l],
             w_ukv[l], gq_b[l], gk_b[l], w_o_a[l], w_o_b[l], w_out[l], g_ffn[l], w_up[l], conv_w[l], conv_b[l],
             w_down[l])
        assert y_prompt.shape[1] == y_sample.shape[1]
        w = _prepare(y_prompt.shape[1], *p)
        y_prompt = _layer(y_prompt, c_prompt, w)
        y_sample = _layer(y_sample, c_sample, w)
    return (y_prompt, y_sample)
```

```python
import functools

import jax
import jax.numpy as jnp
from jax import lax
from jax.experimental import pallas as pl
from jax.experimental.pallas import tpu as pltpu

F32 = jnp.float32
BF16 = jnp.bfloat16

D_MODEL = 1024
A_HEADS = 8
A_KV_HEADS = 2
A_HEAD_DIM = 64
WINDOW = 128
B_HEADS = 8
Q_LORA = 256
KV_LORA = 128
QK_NOPE = 64
QK_ROPE = 32
V_HEAD = 64
D_FF = 2816
ROPE_THETA = 10000.0
EPS = 1e-6
NEG_INF = -1e30

A_Q = A_HEADS * A_HEAD_DIM
A_KV = A_KV_HEADS * A_HEAD_DIM
B_QK = QK_NOPE + QK_ROPE
LANES = 128
B_PAD = B_HEADS * LANES
VT_ROWS = 80

C_QA, C_KA, C_VA, C_CQ, C_CKV, C_KR, C_GA, C_GB, C_END = 0, 512, 640, 768, 1024, 1152, 1280, 2304, 3328

VMEM_LIMIT = 56 * 1024 * 1024
LOG2E = 1.4426950408889634
MAX_FIXED_SHIFT = 50.0
MLA_LOOKAHEAD = 2


def _cparams(n_axes):
    return pltpu.CompilerParams(dimension_semantics=("arbitrary",) * n_axes, vmem_limit_bytes=VMEM_LIMIT)


def _dot(a, b):
    return jnp.dot(a, b, preferred_element_type=F32)


def _dot_nt(a, b):
    return lax.dot_general(a, b, (((1,), (1,)), ((), ())), preferred_element_type=F32)


def _mod_kernel(c_ref, w_ref, b_ref, o_ref):
    c = c_ref[...]
    sc = c * jax.nn.sigmoid(c)
    o_ref[...] = _dot(sc.astype(BF16), w_ref[...].astype(BF16)) + b_ref[...]


def _modulation(c, w_ada, b_ada):
    b = c.shape[0]
    n = w_ada.shape[1]
    tn = 1024
    return pl.pallas_call(
        _mod_kernel,
        grid=(n // tn,),
        in_specs=[
            pl.BlockSpec((b, D_MODEL), lambda j: (0, 0)),
            pl.BlockSpec((D_MODEL, tn), lambda j: (0, j)),
            pl.BlockSpec((1, tn), lambda j: (0, j)),
        ],
        out_specs=pl.BlockSpec((b, tn), lambda j: (0, j)),
        out_shape=jax.ShapeDtypeStruct((b, n), F32),
        compiler_params=_cparams(1),
        name="adaln_mod",
    )(c, w_ada, b_ada.reshape(1, n))


def _group_norm(x, bd, inv_n, gain):
    ss = _dot((x * x).astype(BF16), bd)
    return x * lax.rsqrt(ss * inv_n + EPS) * gain


def _rope(x, c, s_up, s_dn, shift):
    return x * c + pltpu.roll(x, LANES - shift, 1) * s_up + pltpu.roll(x, shift, 1) * s_dn


def _pre_kernel(x_ref, mod_ref, g_ref, w_in_ref, w_uq_ref, w_k_ref, w_v_ref, gcq_ref, gckv_ref,
                gqa_ref, gka_ref, gqb_ref, gkb_ref, bd64_ref, bd128_ref,
                qabias_ref, kabias_ref, qbias_ref, kbias_ref, ca_ref, sua_ref, sda_ref, cb_ref, sub_ref, sdb_ref,
                qa_o, ka_o, va_o, qb_o, kb_o, vt_o, sga_o, sgb_o):
    x = x_ref[0]
    sh1 = mod_ref[0, 0:1, :]
    sc1 = mod_ref[0, 1:2, :]
    h = x * lax.rsqrt(jnp.mean(x * x, axis=-1, keepdims=True) + EPS) * g_ref[...]
    hb = (h * (1.0 + sc1) + sh1).astype(BF16)

    ca, sua, sda = ca_ref[...], sua_ref[...], sda_ref[...]
    cb, sub, sdb = cb_ref[...], sub_ref[...], sdb_ref[...]
    bd64 = bd64_ref[...]
    bd128 = bd128_ref[...]

    t = x.shape[0]
    low_half = lax.broadcasted_iota(jnp.int32, (t, LANES), 1) < A_HEAD_DIM
    row = lax.broadcasted_iota(jnp.int32, (VT_ROWS - V_HEAD, LANES), 0)
    tail_a = jnp.where(row == 0, 1.0, 0.0).astype(BF16)

    def split_heads(pair):
        return jnp.where(low_half, pair, 0.0), jnp.where(low_half, pltpu.roll(pair, A_HEAD_DIM, 1), 0.0)

    zq = _dot(hb, w_in_ref[:, C_QA:C_KA])
    for c in range(2):
        xn = _group_norm(zq[:, 256 * c:256 * (c + 1)], bd64, 1.0 / A_HEAD_DIM, gqa_ref[:, 256 * c:256 * (c + 1)])
        for j in range(2):
            hq = 4 * c + 2 * j
            even, odd = split_heads(_rope(xn[:, LANES * j:LANES * (j + 1)], ca, sua, sda, 32))
            qa_o[0, hq] = (even + qabias_ref[hq]).astype(BF16)
            qa_o[0, hq + 1] = (odd + qabias_ref[hq + 1]).astype(BF16)
    zk = _dot(hb, w_in_ref[:, C_KA:C_VA])
    ssk = _dot((zk * zk).astype(BF16), bd64[0:LANES, 0:LANES])
    kn = zk * lax.rsqrt(ssk * (1.0 / A_HEAD_DIM) + EPS) * gka_ref[...]
    for hk, kh in enumerate(split_heads(_rope(kn, ca, sua, sda, 32))):
        ka_o[0, hk] = (kh + kabias_ref[...]).astype(BF16)
    vat = _dot(hb, w_in_ref[:, C_VA:C_CQ]).T
    for hk in range(A_KV_HEADS):
        for blk in range(t // WINDOW):
            va_o[0, hk, blk, 0:A_HEAD_DIM, :] = vat[A_HEAD_DIM * hk:A_HEAD_DIM * (hk + 1),
                                                    WINDOW * blk:WINDOW * (blk + 1)].astype(BF16)
            va_o[0, hk, blk, A_HEAD_DIM:VT_ROWS, :] = tail_a

    cq = _dot(hb, w_in_ref[:, C_CQ:C_CKV])
    cqn = (cq * lax.rsqrt(jnp.mean(cq * cq, axis=-1, keepdims=True) + EPS) * gcq_ref[...]).astype(BF16)
    ckv = _dot(hb, w_in_ref[:, C_CKV:C_KR])
    ckvn = (ckv * lax.rsqrt(jnp.mean(ckv * ckv, axis=-1, keepdims=True) + EPS) * gckv_ref[...]).astype(BF16)
    krp = _dot(hb, w_in_ref[:, C_KR:C_GA]).astype(BF16)

    qraw = _dot(cqn, w_uq_ref[...])
    kraw = _dot(jnp.concatenate([ckvn, krp], axis=1), w_k_ref[...])
    for c in range(4):
        sl = slice(256 * c, 256 * (c + 1))
        qn = _group_norm(qraw[:, sl], bd128, 1.0 / B_QK, gqb_ref[:, sl])
        kn2 = _group_norm(kraw[:, sl], bd128, 1.0 / B_QK, gkb_ref[:, sl])
        for j in range(2):
            hh = 2 * c + j
            ls = slice(LANES * j, LANES * (j + 1))
            qb_o[0, hh] = (_rope(qn[:, ls], cb, sub, sdb, 16) + qbias_ref[...]).astype(BF16)
            kb_o[0, hh] = (_rope(kn2[:, ls], cb, sub, sdb, 16) + kbias_ref[...]).astype(BF16)

    vv = _dot(ckvn, w_v_ref[...])
    vt = vv.T
    row = lax.broadcasted_iota(jnp.int32, (VT_ROWS - V_HEAD, t), 0)
    tail = jnp.where(row == 0, 1.0, 0.0).astype(BF16)
    for hh in range(B_HEADS):
        vt_o[0, hh, 0, 0:V_HEAD, :] = vt[V_HEAD * hh:V_HEAD * (hh + 1), :].astype(BF16)
        vt_o[0, hh, 0, V_HEAD:VT_ROWS, :] = tail

    sga_o[0] = jax.nn.sigmoid(_dot(hb, w_in_ref[:, C_GA:C_GB])).astype(BF16)
    sgb_o[0] = jax.nn.sigmoid(_dot(hb, w_in_ref[:, C_GB:C_END])).astype(BF16)


def _pre_attention(x, mod, w, t):
    b, s, _ = x.shape
    nt = s // t
    const2 = lambda i, bb: (0, 0)
    tok2 = lambda i, bb: (i, 0)
    in_specs = [
        pl.BlockSpec((1, t, D_MODEL), lambda i, bb: (bb, i, 0)),
        pl.BlockSpec((1, 6, D_MODEL), lambda i, bb: (bb, 0, 0)),
        pl.BlockSpec((1, D_MODEL), const2),
        pl.BlockSpec((D_MODEL, C_END), const2),
        pl.BlockSpec((Q_LORA, B_PAD), const2),
        pl.BlockSpec((256, B_PAD), const2),
        pl.BlockSpec((KV_LORA, B_HEADS * V_HEAD), const2),
        pl.BlockSpec((1, Q_LORA), const2),
        pl.BlockSpec((1, KV_LORA), const2),
        pl.BlockSpec((1, A_Q), const2),
        pl.BlockSpec((1, A_KV), const2),
        pl.BlockSpec((1, B_PAD), const2),
        pl.BlockSpec((1, B_PAD), const2),
        pl.BlockSpec((256, 256), const2),
        pl.BlockSpec((256, 256), const2),
        pl.BlockSpec((A_HEADS, 1, LANES), lambda i, bb: (0, 0, 0)),
        pl.BlockSpec((1, LANES), const2),
        pl.BlockSpec((1, LANES), const2),
        pl.BlockSpec((1, LANES), const2),
    ] + [pl.BlockSpec((t, LANES), tok2)] * 6
    out_shape = [
        jax.ShapeDtypeStruct((b, A_HEADS, s, LANES), BF16),
        jax.ShapeDtypeStruct((b, A_KV_HEADS, s, LANES), BF16),
        jax.ShapeDtypeStruct((b, A_KV_HEADS, s // WINDOW, VT_ROWS, WINDOW), BF16),
        jax.ShapeDtypeStruct((b, B_HEADS, s, LANES), BF16),
        jax.ShapeDtypeStruct((b, B_HEADS, s, LANES), BF16),
        jax.ShapeDtypeStruct((b, B_HEADS, nt, VT_ROWS, t), BF16),
        jax.ShapeDtypeStruct((b, s, D_MODEL), BF16),
        jax.ShapeDtypeStruct((b, s, D_MODEL), BF16),
    ]
    tok3 = lambda i, bb: (bb, i, 0)
    out_specs = [
        pl.BlockSpec((1, A_HEADS, t, LANES), lambda i, bb: (bb, 0, i, 0)),
        pl.BlockSpec((1, A_KV_HEADS, t, LANES), lambda i, bb: (bb, 0, i, 0)),
        pl.BlockSpec((1, A_KV_HEADS, t // WINDOW, VT_ROWS, WINDOW), lambda i, bb: (bb, 0, i, 0, 0)),
        pl.BlockSpec((1, B_HEADS, t, LANES), lambda i, bb: (bb, 0, i, 0)),
        pl.BlockSpec((1, B_HEADS, t, LANES), lambda i, bb: (bb, 0, i, 0)),
        pl.BlockSpec((1, B_HEADS, 1, VT_ROWS, t), lambda i, bb: (bb, 0, i, 0, 0)),
        pl.BlockSpec((1, t, D_MODEL), tok3),
        pl.BlockSpec((1, t, D_MODEL), tok3),
    ]
    return pl.pallas_call(
        _pre_kernel,
        grid=(nt, b),
        in_specs=in_specs,
        out_specs=out_specs,
        out_shape=out_shape,
        compiler_params=_cparams(2),
        name="pre_attention",
    )(x, mod, w["g_attn"], w["w_in"], w["w_uq"], w["w_k"], w["w_v"], w["g_cq"], w["g_ckv"],
      w["gq_a"], w["gk_a"], w["gq_b"], w["gk_b"], w["bd64"], w["bd128"], w["qabias"], w["kabias"],
      w["qbias"], w["kbias"],
      w["ca"], w["sua"], w["sda"], w["cb"], w["sub"], w["sdb"])


def _window_kernel(r, fast, sink_ref, q_ref, kp_ref, km_ref, kn_ref, vp_ref, vm_ref, vn_ref, o_ref,
                   kcat, bias_scr, o_scr):
    i = pl.program_id(1)
    last_i = pl.num_programs(1) - 1
    blk = WINDOW
    grp = A_HEADS // A_KV_HEADS
    nq = grp * blk

    @pl.when(i == 0)
    def _():
        kk = lax.broadcasted_iota(jnp.int32, (3 * blk, nq), 0)
        qq = lax.broadcasted_iota(jnp.int32, (3 * blk, nq), 1) & (blk - 1)
        base = jnp.where(kk >= qq, jnp.where(kk <= qq + 2 * blk, 0.0, NEG_INF), NEG_INF)
        bias_scr[0] = base
        bias_scr[1] = jnp.where(kk >= blk, base, NEG_INF)
        bias_scr[2] = jnp.where(kk < 2 * blk, base, NEG_INF)

    for hk in range(A_KV_HEADS):
        kcat[hk, 0:blk] = kp_ref[0, hk]
        kcat[hk, blk:(r + 1) * blk] = km_ref[0, hk]
        kcat[hk, (r + 1) * blk:(r + 2) * blk] = kn_ref[0, hk]
    units = [(j, hk) for j in range(r) for hk in range(A_KV_HEADS)]

    def scores(u):
        j, hk = units[u]
        q4 = jnp.concatenate([q_ref[0, hk * grp + g, j * blk:(j + 1) * blk, :] for g in range(grp)], axis=0)
        return _dot_nt(kcat[hk, j * blk:(j + 3) * blk, :], q4)

    pending = {0: scores(0)}
    for u, (j, hk) in enumerate(units):
        if u + 1 < len(units):
            pending[u + 1] = scores(u + 1)
        if j == 0:
            bias = bias_scr[jnp.where(i == 0, 1, 0)]
        elif j == r - 1:
            bias = bias_scr[jnp.where(i == last_i, 2, 0)]
        else:
            bias = bias_scr[0]
        st = pending.pop(u) + bias
        if fast:
            p = jnp.exp2(st).astype(BF16)
            sink_term = sink_ref[hk]
        else:
            srow = sink_ref[hk]
            m = jnp.maximum(jnp.max(st, axis=0, keepdims=True), srow)
            p = jnp.exp2(st - m).astype(BF16)
            sink_term = jnp.exp2(srow - m)
        vblk = [vp_ref[0, hk, 0]] + [vm_ref[0, hk, n] for n in range(r)] + [vn_ref[0, hk, 0]]
        acc = _dot(jnp.concatenate(vblk[j:j + 3], axis=1), p)
        ot = acc[0:A_HEAD_DIM] / (acc[A_HEAD_DIM:A_HEAD_DIM + 1] + sink_term)
        for g in range(grp):
            hq = hk * grp + g
            o_scr[hq * A_HEAD_DIM:(hq + 1) * A_HEAD_DIM, j * blk:(j + 1) * blk] = ot[:, g * blk:(g + 1) * blk]
    o_ref[0] = o_scr[...].T.astype(BF16)


def _window_attention(qa, ka, vat, sink_rows, tq, fast):
    b, _, s, _ = qa.shape
    r = tq // WINDOW
    nblk = s // WINDOW
    assert r >= 2
    grp = A_HEADS // A_KV_HEADS
    prev = lambda i: jnp.maximum(i * r - 1, 0)
    nxt = lambda i: jnp.minimum((i + 1) * r, nblk - 1)
    k_specs = [
        pl.BlockSpec((1, A_KV_HEADS, WINDOW, LANES), lambda bb, i: (bb, 0, prev(i), 0)),
        pl.BlockSpec((1, A_KV_HEADS, tq, LANES), lambda bb, i: (bb, 0, i, 0)),
        pl.BlockSpec((1, A_KV_HEADS, WINDOW, LANES), lambda bb, i: (bb, 0, nxt(i), 0)),
    ]
    v_specs = [
        pl.BlockSpec((1, A_KV_HEADS, 1, VT_ROWS, WINDOW), lambda bb, i: (bb, 0, prev(i), 0, 0)),
        pl.BlockSpec((1, A_KV_HEADS, r, VT_ROWS, WINDOW), lambda bb, i: (bb, 0, i, 0, 0)),
        pl.BlockSpec((1, A_KV_HEADS, 1, VT_ROWS, WINDOW), lambda bb, i: (bb, 0, nxt(i), 0, 0)),
    ]
    return pl.pallas_call(
        functools.partial(_window_kernel, r, fast),
        grid=(b, s // tq),
        in_specs=[pl.BlockSpec((A_KV_HEADS, 1, grp * WINDOW), lambda bb, i: (0, 0, 0)),
                  pl.BlockSpec((1, A_HEADS, tq, LANES), lambda bb, i: (bb, 0, i, 0))] + k_specs + v_specs,
        out_specs=pl.BlockSpec((1, tq, A_Q), lambda bb, i: (bb, i, 0)),
        out_shape=jax.ShapeDtypeStruct((b, s, A_Q), BF16),
        scratch_shapes=[pltpu.VMEM((A_KV_HEADS, tq + 2 * WINDOW, LANES), BF16),
                        pltpu.VMEM((3, 3 * WINDOW, grp * WINDOW), F32),
                        pltpu.VMEM((A_Q, tq), F32)],
        compiler_params=_cparams(2),
        name="window_attention_fast" if fast else "window_attention_safe",
    )(sink_rows, qa, ka, ka, ka, vat, vat, vat)


def _mla_safe_kernel(nc, q_ref, k_ref, vt_ref, o_ref, s_scr, o_scr):
    tk = vt_ref.shape[-1]
    tq = q_ref.shape[2]

    def head(h, carry):
        q = q_ref[0, h]

        def pass1(c, m):
            kc = k_ref[0, h, pl.ds(pl.multiple_of(c * tk, tk), tk), :]
            st = _dot_nt(kc, q)
            s_scr[c] = st
            return jnp.maximum(m, jnp.max(st, axis=0, keepdims=True))

        m = lax.fori_loop(0, nc, pass1, jnp.full((1, tq), -jnp.inf, F32))

        def pass2(c, acc):
            p = jnp.exp2(s_scr[c] - m).astype(BF16)
            return acc + _dot(vt_ref[0, h, c], p)

        acc = lax.fori_loop(0, nc, pass2, jnp.zeros((VT_ROWS, tq), F32))
        o_scr[pl.ds(pl.multiple_of(h * V_HEAD, V_HEAD), V_HEAD), :] = acc[0:V_HEAD] / acc[V_HEAD:V_HEAD + 1]
        return carry

    lax.fori_loop(0, B_HEADS, head, 0)
    o_ref[0] = o_scr[...].T.astype(BF16)


def _mla_fast_kernel(nc, q_ref, k_ref, vt_ref, o_ref, o_scr):
    tk = vt_ref.shape[-1]
    steps = [(h, c) for h in range(B_HEADS) for c in range(nc)]

    def scores(t):
        h, c = steps[t]
        return _dot_nt(k_ref[0, h, c * tk:(c + 1) * tk, :], q_ref[0, h])

    pending = {t: scores(t) for t in range(MLA_LOOKAHEAD)}
    acc = None
    for t, (h, c) in enumerate(steps):
        if t + MLA_LOOKAHEAD < len(steps):
            pending[t + MLA_LOOKAHEAD] = scores(t + MLA_LOOKAHEAD)
        p = jnp.exp2(pending.pop(t)).astype(BF16)
        d = _dot(vt_ref[0, h, c], p)
        acc = d if c == 0 else acc + d
        if c == nc - 1:
            o_scr[h * V_HEAD:(h + 1) * V_HEAD, :] = acc[0:V_HEAD] / acc[V_HEAD:V_HEAD + 1]
    o_ref[0] = o_scr[...].T.astype(BF16)


def _mla_attention(qb, kb, vt, tq, fast):
    b, _, s, _ = qb.shape
    nc, tk = vt.shape[2], vt.shape[4]
    scratch = [pltpu.VMEM((B_HEADS * V_HEAD, tq), F32)]
    if fast:
        body = functools.partial(_mla_fast_kernel, nc)
    else:
        body = functools.partial(_mla_safe_kernel, nc)
        scratch = [pltpu.VMEM((nc, tk, tq), F32)] + scratch
    return pl.pallas_call(
        body,
        grid=(b, s // tq),
        in_specs=[
            pl.BlockSpec((1, B_HEADS, tq, LANES), lambda bb, i: (bb, 0, i, 0)),
            pl.BlockSpec((1, B_HEADS, s, LANES), lambda bb, i: (bb, 0, 0, 0)),
            pl.BlockSpec((1, B_HEADS, nc, VT_ROWS, tk), lambda bb, i: (bb, 0, 0, 0, 0)),
        ],
        out_specs=pl.BlockSpec((1, tq, B_HEADS * V_HEAD), lambda bb, i: (bb, i, 0)),
        out_shape=jax.ShapeDtypeStruct((b, s, B_HEADS * V_HEAD), BF16),
        scratch_shapes=scratch,
        compiler_params=_cparams(2),
        name="mla_attention_fast" if fast else "mla_attention_safe",
    )(qb, kb, vt)


def _merge_kernel(x_ref, mod_ref, oa_ref, ob_ref, sga_ref, sgb_ref, woa_ref, wob_ref, wout_ref, g_ref,
                  x1_o, h2_o):
    gt1 = mod_ref[0, 2:3, :]
    sh2 = mod_ref[0, 3:4, :]
    sc2 = mod_ref[0, 4:5, :]
    mix = (sga_ref[0].astype(F32) * _dot(oa_ref[0], woa_ref[...])
           + sgb_ref[0].astype(F32) * _dot(ob_ref[0], wob_ref[...]))
    x1 = x_ref[0] + gt1 * _dot(mix.astype(BF16), wout_ref[...])
    x1_o[0] = x1
    h2 = x1 * lax.rsqrt(jnp.mean(x1 * x1, axis=-1, keepdims=True) + EPS) * g_ref[...]
    h2_o[0] = (h2 * (1.0 + sc2) + sh2).astype(BF16)


def _merge(x, mod, oa, ob, sga, sgb, w, t):
    b, s, _ = x.shape
    tok = lambda i, bb: (bb, i, 0)
    const2 = lambda i, bb: (0, 0)
    return pl.pallas_call(
        _merge_kernel,
        grid=(s // t, b),
        in_specs=[
            pl.BlockSpec((1, t, D_MODEL), tok),
            pl.BlockSpec((1, 6, D_MODEL), lambda i, bb: (bb, 0, 0)),
            pl.BlockSpec((1, t, A_Q), tok),
            pl.BlockSpec((1, t, B_HEADS * V_HEAD), tok),
            pl.BlockSpec((1, t, D_MODEL), tok),
            pl.BlockSpec((1, t, D_MODEL), tok),
            pl.BlockSpec((A_Q, D_MODEL), const2),
            pl.BlockSpec((B_HEADS * V_HEAD, D_MODEL), const2),
            pl.BlockSpec((D_MODEL, D_MODEL), const2),
            pl.BlockSpec((1, D_MODEL), const2),
        ],
        out_specs=[pl.BlockSpec((1, t, D_MODEL), tok), pl.BlockSpec((1, t, D_MODEL), tok)],
        out_shape=[jax.ShapeDtypeStruct((b, s, D_MODEL), F32), jax.ShapeDtypeStruct((b, s, D_MODEL), BF16)],
        compiler_params=_cparams(2),
        name="merge_outproj",
    )(x, mod, oa, ob, sga, sgb, w["w_o_a"], w["w_o_b"], w["w_out"], w["g_ffn"])


MXU_DIM = 256
FF_SPLIT = (D_FF // MXU_DIM + 1) // 2 * MXU_DIM
FF_CHUNKS = ((0, FF_SPLIT), (FF_SPLIT, D_FF))
HALO = 16


def _ffn_kernel(x1_ref, mod_ref, hp_ref, hm_ref, hn_ref, wup_ref, cw_ref, cb_ref, wdn_ref, y_o,
                hcat, ua_scr, ug_scr):
    i = pl.program_id(0)
    last = pl.num_programs(0) - 1
    t = hm_ref.shape[1]
    gt2 = mod_ref[0, 5:6, :]
    hcat[0:HALO] = jnp.where(i > 0, hp_ref[0], jnp.zeros_like(hp_ref[0]))
    hcat[HALO:HALO + t] = hm_ref[0]
    hcat[HALO + t:HALO + t + HALO] = jnp.where(i < last, hn_ref[0], jnp.zeros_like(hn_ref[0]))
    hc = hcat[...]
    acc = None
    for c0, c1 in FF_CHUNKS:
        n = c1 - c0
        branch = []
        for off, scr in ((0, ua_scr), (D_FF, ug_scr)):
            lo = off + c0
            scr[:, 0:n] = _dot(hc, wup_ref[:, lo:lo + n])
            u = cb_ref[:, lo:lo + n]
            for j in range(3):
                u = u + scr[pl.ds(HALO - 1 + j, t), 0:n] * cw_ref[j:j + 1, lo:lo + n]
            branch.append(u)
        ua, ug = branch
        act = (ug * jax.nn.sigmoid(ug) * ua).astype(BF16)
        d = _dot(act, wdn_ref[c0:c1, :])
        acc = d if acc is None else acc + d
    y_o[0] = x1_ref[0] + gt2 * acc


def _ffn(x1, mod, h2, w, t):
    b, s, _ = x1.shape
    r = t // HALO
    nh = s // HALO
    tok = lambda i, bb: (bb, i, 0)
    const2 = lambda i, bb: (0, 0)
    return pl.pallas_call(
        _ffn_kernel,
        grid=(s // t, b),
        in_specs=[
            pl.BlockSpec((1, t, D_MODEL), tok),
            pl.BlockSpec((1, 6, D_MODEL), lambda i, bb: (bb, 0, 0)),
            pl.BlockSpec((1, HALO, D_MODEL), lambda i, bb: (bb, jnp.maximum(i * r - 1, 0), 0)),
            pl.BlockSpec((1, t, D_MODEL), tok),
            pl.BlockSpec((1, HALO, D_MODEL), lambda i, bb: (bb, jnp.minimum((i + 1) * r, nh - 1), 0)),
            pl.BlockSpec((D_MODEL, 2 * D_FF), const2, pipeline_mode=pl.Buffered(1)),
            pl.BlockSpec((3, 2 * D_FF), const2),
            pl.BlockSpec((1, 2 * D_FF), const2),
            pl.BlockSpec((D_FF, D_MODEL), const2, pipeline_mode=pl.Buffered(1)),
        ],
        out_specs=pl.BlockSpec((1, t, D_MODEL), tok),
        out_shape=jax.ShapeDtypeStruct((b, s, D_MODEL), F32),
        scratch_shapes=[
            pltpu.VMEM((t + 2 * HALO, D_MODEL), BF16),
            pltpu.VMEM((t + 2 * HALO, FF_SPLIT), F32),
            pltpu.VMEM((t + 2 * HALO, FF_SPLIT), F32),
        ],
        compiler_params=_cparams(2),
        name="conv_ffn",
    )(x1, mod, h2, h2, h2, w["w_up"], w["conv_w"], w["conv_b"], w["w_down"])


def _rope_tables(seq, dim, n_rep, lead, pad):
    half = dim // 2
    inv = 1.0 / (ROPE_THETA ** (jnp.arange(0, dim, 2, dtype=F32) / dim))
    ang = jnp.arange(seq, dtype=F32)[:, None] * inv[None, :]
    cos, sin = jnp.cos(ang), jnp.sin(ang)
    zero = jnp.zeros_like(sin)
    c = jnp.concatenate([jnp.ones((seq, lead), F32)] + [cos, cos] * n_rep + [jnp.zeros((seq, pad), F32)], axis=1)
    s_up = jnp.concatenate([jnp.zeros((seq, lead), F32)] + [-sin, zero] * n_rep + [jnp.zeros((seq, pad), F32)], axis=1)
    s_dn = jnp.concatenate([jnp.zeros((seq, lead), F32)] + [zero, sin] * n_rep + [jnp.zeros((seq, pad), F32)], axis=1)
    del half
    return c, s_up, s_dn


def _prepare(seq, w_ada, b_ada, g_attn, w_in, gq_a, gk_a, sink_a, g_cq, w_uq, g_ckv, w_ukv, gq_b, gk_b,
             w_o_a, w_o_b, w_out, g_ffn, w_up, conv_w, conv_b, w_down):
    kr_lo = A_Q + 2 * A_KV + Q_LORA + KV_LORA
    kr_hi = kr_lo + QK_ROPE
    w_in_p = jnp.concatenate(
        [w_in[:, :kr_hi], jnp.zeros((D_MODEL, LANES - QK_ROPE), F32), w_in[:, kr_hi:]], axis=1).astype(BF16)
    pad_h = lambda a: jnp.pad(a, ((0, 0), (0, 0), (0, LANES - a.shape[2]))).reshape(a.shape[0], B_PAD)
    w_uq_p = pad_h(w_uq.reshape(Q_LORA, B_HEADS, B_QK)).astype(BF16)
    ukv = w_ukv.reshape(KV_LORA, B_HEADS, QK_NOPE + V_HEAD)
    wk_nope = pad_h(ukv[:, :, :QK_NOPE])
    eye = jnp.eye(QK_ROPE, dtype=F32)
    e_rope = jnp.pad(eye, ((0, 0), (QK_NOPE, LANES - B_QK)))
    wk_rope = jnp.tile(e_rope, (1, B_HEADS))
    w_k = jnp.concatenate([wk_nope, wk_rope, jnp.zeros((256 - KV_LORA - QK_ROPE, B_PAD), F32)], axis=0).astype(BF16)
    w_v = ukv[:, :, QK_NOPE:].reshape(KV_LORA, B_HEADS * V_HEAD).astype(BF16)
    pad_g = lambda g: jnp.tile(jnp.pad(g, (0, LANES - B_QK)), B_HEADS).reshape(1, B_PAD)
    idx = jnp.arange(256)
    bd = lambda n: (idx[:, None] // n == idx[None, :] // n).astype(BF16)
    ca, sua, sda = _rope_tables(seq, A_HEAD_DIM, 2, 0, 0)
    cb, sub, sdb = _rope_tables(seq, QK_ROPE, 1, QK_NOPE, LANES - B_QK)
    bound = (B_QK ** 0.5) * LOG2E * jnp.max(jnp.abs(gq_b)) * jnp.max(jnp.abs(gk_b))
    shift = jnp.ceil(bound * 1.02) + 1.0
    lane = jnp.arange(LANES)
    qbias = jnp.where(lane == B_QK, 1.0, 0.0).astype(F32).reshape(1, LANES)
    kbias = jnp.where(lane == B_QK, -shift, 0.0).astype(F32).reshape(1, LANES)
    bound_a = (A_HEAD_DIM ** 0.5) * LOG2E * jnp.max(jnp.abs(gq_a)) * jnp.max(jnp.abs(gk_a))
    shift_a0 = jnp.ceil(bound_a * 1.02) + 1.0
    sink2 = sink_a * LOG2E
    shift_a = jnp.maximum(shift_a0, sink2).astype(BF16).astype(F32)
    qabias = jnp.where(lane[None, :] == A_HEAD_DIM, -shift_a[:, None], 0.0).reshape(A_HEADS, 1, LANES)
    kabias = jnp.where(lane == A_HEAD_DIM, 1.0, 0.0).astype(F32).reshape(1, LANES)
    rows = lambda v: jnp.repeat(v, WINDOW).reshape(A_KV_HEADS, 1, A_HEADS // A_KV_HEADS * WINDOW)
    return {
        "shift": shift, "qbias": qbias, "kbias": kbias,
        "shift_a": shift_a0, "qabias": qabias, "kabias": kabias,
        "sink_fast": rows(jnp.exp2(sink2 - shift_a)), "sink_safe": rows(sink2 - shift_a),
        "w_ada": w_ada, "b_ada": b_ada,
        "g_attn": g_attn.reshape(1, D_MODEL),
        "w_in": w_in_p, "w_uq": w_uq_p, "w_k": w_k, "w_v": w_v,
        "g_cq": g_cq.reshape(1, Q_LORA), "g_ckv": g_ckv.reshape(1, KV_LORA),
        "gq_a": (jnp.tile(gq_a, A_HEADS) * (A_HEAD_DIM ** -0.5 * LOG2E)).reshape(1, A_Q),
        "gk_a": jnp.tile(gk_a, A_KV_HEADS).reshape(1, A_KV),
        "gq_b": pad_g(gq_b) * (B_QK ** -0.5 * LOG2E), "gk_b": pad_g(gk_b),
        "bd64": bd(A_HEAD_DIM), "bd128": bd(LANES),
        "ca": ca, "sua": sua, "sda": sda, "cb": cb, "sub": sub, "sdb": sdb,
        "w_o_a": w_o_a.astype(BF16), "w_o_b": w_o_b.astype(BF16), "w_out": w_out.astype(BF16),
        "g_ffn": g_ffn.reshape(1, D_MODEL),
        "w_up": w_up.astype(BF16), "conv_w": conv_w, "conv_b": conv_b.reshape(1, 2 * D_FF),
        "w_down": w_down.astype(BF16),
    }


def _tile(s, pref):
    t = min(pref, s)
    assert s % t == 0 and t % WINDOW == 0, (s, t)
    return t


def _layer(x, c, w):
    b, s, _ = x.shape
    mod = _modulation(c, w["w_ada"], w["b_ada"]).reshape(b, 6, D_MODEL)
    qa, ka, va, qb, kb, vt, sga, sgb = _pre_attention(x, mod, w, _tile(s, 256))
    tw = _tile(s, 512)
    oa = lax.cond(w["shift_a"] <= MAX_FIXED_SHIFT,
                  lambda *a: _window_attention(*a, w["sink_fast"], tw, True),
                  lambda *a: _window_attention(*a, w["sink_safe"], tw, False), qa, ka, va)
    tq = _tile(s, 512)
    ob = lax.cond(w["shift"] <= MAX_FIXED_SHIFT,
                  lambda *a: _mla_attention(*a, tq, True), lambda *a: _mla_attention(*a, tq, False), qb, kb, vt)
    x1, h2 = _merge(x, mod, oa, ob, sga, sgb, w, _tile(s, 512))
    return _ffn(x1, mod, h2, w, _tile(s, 512))


def kernel(x_prompt, x_sample, c_prompt, c_sample, w_ada, b_ada, g_attn, w_in, gq_a, gk_a, sink_a, g_cq, w_uq,
           g_ckv, w_ukv, gq_b, gk_b, w_o_a, w_o_b, w_out, g_ffn, w_up, conv_w, conv_b, w_down):
    y_prompt, y_sample = x_prompt, x_sample
    for l in range(w_ada.shape[0]):
        p = (w_ada[l], b_ada[l], g_attn[l], w_in[l], gq_a[l], gk_a[l], sink_a[l], g_cq[l], w_uq[l], g_ckv[l],
             w_ukv[l], gq_b[l], gk_b[l], w_o_a[l], w_o_b[l], w_out[l], g_ffn[l], w_up[l], conv_w[l], conv_b[l],
             w_down[l])
        assert y_prompt.shape[1] == y_sample.shape[1]
        w = _prepare(y_prompt.shape[1], *p)
        y_prompt = _layer(y_prompt, c_prompt, w)
        y_sample = _layer(y_sample, c_sample, w)
    return (y_prompt, y_sample)
```

```python
import functools

import jax
import jax.numpy as jnp
from jax import lax
from jax.experimental import pallas as pl
from jax.experimental.pallas import tpu as pltpu

F32 = jnp.float32
BF16 = jnp.bfloat16

D_MODEL = 1024
A_HEADS = 8
A_KV_HEADS = 2
A_HEAD_DIM = 64
WINDOW = 128
B_HEADS = 8
Q_LORA = 256
KV_LORA = 128
QK_NOPE = 64
QK_ROPE = 32
V_HEAD = 64
D_FF = 2816
ROPE_THETA = 10000.0
EPS = 1e-6
NEG_INF = -1e30

A_Q = A_HEADS * A_HEAD_DIM
A_KV = A_KV_HEADS * A_HEAD_DIM
B_QK = QK_NOPE + QK_ROPE
LANES = 128
B_PAD = B_HEADS * LANES
VT_ROWS = 80

C_QA, C_KA, C_VA, C_CQ, C_CKV, C_KR, C_END = 0, 512, 640, 768, 1024, 1152, 1280

VMEM_LIMIT = 56 * 1024 * 1024
LOG2E = 1.4426950408889634
MAX_FIXED_SHIFT = 50.0
MLA_LOOKAHEAD = 4
PRE_SUB = 256


def _cparams(n_axes):
    return pltpu.CompilerParams(dimension_semantics=("arbitrary",) * n_axes, vmem_limit_bytes=VMEM_LIMIT)


def _dot(a, b):
    return jnp.dot(a, b, preferred_element_type=F32)


def _dot_nt(a, b):
    return lax.dot_general(a, b, (((1,), (1,)), ((), ())), preferred_element_type=F32)


def _mod_kernel(c_ref, w_ref, b_ref, o_ref):
    c = c_ref[...]
    sc = c * jax.nn.sigmoid(c)
    o_ref[...] = _dot(sc.astype(BF16), w_ref[...].astype(BF16)) + b_ref[...]


def _modulation(c, w_ada, b_ada):
    b = c.shape[0]
    n = w_ada.shape[1]
    tn = 1024
    return pl.pallas_call(
        _mod_kernel,
        grid=(n // tn,),
        in_specs=[
            pl.BlockSpec((b, D_MODEL), lambda j: (0, 0)),
            pl.BlockSpec((D_MODEL, tn), lambda j: (0, j)),
            pl.BlockSpec((1, tn), lambda j: (0, j)),
        ],
        out_specs=pl.BlockSpec((b, tn), lambda j: (0, j)),
        out_shape=jax.ShapeDtypeStruct((b, n), F32),
        compiler_params=_cparams(1),
        name="adaln_mod",
    )(c, w_ada, b_ada.reshape(1, n))


def _group_norm(x, bd, inv_n, gain):
    ss = _dot((x * x).astype(BF16), bd)
    return x * lax.rsqrt(ss * inv_n + EPS) * gain


def _rope(x, c, s_up, s_dn, shift):
    return x * c + pltpu.roll(x, LANES - shift, 1) * s_up + pltpu.roll(x, shift, 1) * s_dn


def _pre_kernel(x_ref, mod_ref, g_ref, w_in_ref, w_uq_ref, w_uqs_ref, w_k_ref, w_ks_ref, w_v_ref, gcq_ref, gckv_ref,
                gqa_ref, gka_ref, gqb_ref, gqbs_ref, gkb_ref, gkbs_ref, bd64_ref, bd128_ref,
                qabias_ref, kabias_ref, qbias_ref, kbias_ref, ca_ref, sua_ref, sda_ref, cb_ref, sb_ref,
                qa_o, ka_o, va_o, qb_o, kb_o, vt_o):
    for sub in range(x_ref.shape[1] // PRE_SUB):
        _pre_rows(sub, x_ref, mod_ref, g_ref, w_in_ref, w_uq_ref, w_uqs_ref, w_k_ref, w_ks_ref, w_v_ref, gcq_ref,
                  gckv_ref, gqa_ref, gka_ref, gqb_ref, gqbs_ref, gkb_ref, gkbs_ref, bd64_ref, bd128_ref,
                  qabias_ref, kabias_ref, qbias_ref, kbias_ref, ca_ref, sua_ref, sda_ref, cb_ref, sb_ref,
                  qa_o, ka_o, va_o, qb_o, kb_o, vt_o)


def _pre_rows(sub, x_ref, mod_ref, g_ref, w_in_ref, w_uq_ref, w_uqs_ref, w_k_ref, w_ks_ref, w_v_ref, gcq_ref,
              gckv_ref, gqa_ref, gka_ref, gqb_ref, gqbs_ref, gkb_ref, gkbs_ref, bd64_ref, bd128_ref,
              qabias_ref, kabias_ref, qbias_ref, kbias_ref, ca_ref, sua_ref, sda_ref, cb_ref, sb_ref,
              qa_o, ka_o, va_o, qb_o, kb_o, vt_o):
    t = PRE_SUB
    rows = slice(sub * t, (sub + 1) * t)
    x = x_ref[0, rows]
    sh1 = mod_ref[0, 0:1, :]
    sc1 = mod_ref[0, 1:2, :]
    h = x * lax.rsqrt(jnp.mean(x * x, axis=-1, keepdims=True) + EPS) * g_ref[...]
    hb = (h * (1.0 + sc1) + sh1).astype(BF16)

    ca, sua, sda = ca_ref[rows], sua_ref[rows], sda_ref[rows]
    cb, sb = cb_ref[rows], sb_ref[rows]
    bd64 = bd64_ref[...]
    bd128 = bd128_ref[...]

    low_half = lax.broadcasted_iota(jnp.int32, (t, LANES), 1) < A_HEAD_DIM
    row = lax.broadcasted_iota(jnp.int32, (VT_ROWS - V_HEAD, LANES), 0)
    tail_a = jnp.where(row == 0, 1.0, 0.0).astype(BF16)

    def split_heads(pair):
        return jnp.where(low_half, pair, 0.0), jnp.where(low_half, pltpu.roll(pair, A_HEAD_DIM, 1), 0.0)

    zq = _dot(hb, w_in_ref[:, C_QA:C_KA])
    for c in range(2):
        xn = _group_norm(zq[:, 256 * c:256 * (c + 1)], bd64, 1.0 / A_HEAD_DIM, gqa_ref[:, 256 * c:256 * (c + 1)])
        for j in range(2):
            hq = 4 * c + 2 * j
            even, odd = split_heads(_rope(xn[:, LANES * j:LANES * (j + 1)], ca, sua, sda, 32))
            qa_o[0, hq, rows] = (even + qabias_ref[hq]).astype(BF16)
            qa_o[0, hq + 1, rows] = (odd + qabias_ref[hq + 1]).astype(BF16)
    zk = _dot(hb, w_in_ref[:, C_KA:C_VA])
    ssk = _dot((zk * zk).astype(BF16), bd64[0:LANES, 0:LANES])
    kn = zk * lax.rsqrt(ssk * (1.0 / A_HEAD_DIM) + EPS) * gka_ref[...]
    for hk, kh in enumerate(split_heads(_rope(kn, ca, sua, sda, 32))):
        ka_o[0, hk, rows] = (kh + kabias_ref[...]).astype(BF16)
    vat = _dot(hb, w_in_ref[:, C_VA:C_CQ]).T
    for hk in range(A_KV_HEADS):
        for blk in range(t // WINDOW):
            ob = sub * (t // WINDOW) + blk
            va_o[0, hk, ob, 0:A_HEAD_DIM, :] = vat[A_HEAD_DIM * hk:A_HEAD_DIM * (hk + 1),
                                                   WINDOW * blk:WINDOW * (blk + 1)].astype(BF16)
            va_o[0, hk, ob, A_HEAD_DIM:VT_ROWS, :] = tail_a

    cq = _dot(hb, w_in_ref[:, C_CQ:C_CKV])
    cqn = (cq * lax.rsqrt(jnp.mean(cq * cq, axis=-1, keepdims=True) + EPS) * gcq_ref[...]).astype(BF16)
    ckv = _dot(hb, w_in_ref[:, C_CKV:C_KR])
    ckvn = (ckv * lax.rsqrt(jnp.mean(ckv * ckv, axis=-1, keepdims=True) + EPS) * gckv_ref[...]).astype(BF16)
    krp = _dot(hb, w_in_ref[:, C_KR:C_END]).astype(BF16)

    k_lhs = jnp.concatenate([ckvn, krp], axis=1)
    qraw, qpar = _dot(cqn, w_uq_ref[...]), _dot(cqn, w_uqs_ref[...])
    kraw, kpar = _dot(k_lhs, w_k_ref[...]), _dot(k_lhs, w_ks_ref[...])
    for c in range(4):
        sl = slice(256 * c, 256 * (c + 1))
        rq = lax.rsqrt(_dot((qraw[:, sl] * qraw[:, sl]).astype(BF16), bd128) * (1.0 / B_QK) + EPS)
        rk = lax.rsqrt(_dot((kraw[:, sl] * kraw[:, sl]).astype(BF16), bd128) * (1.0 / B_QK) + EPS)
        qn, qp = qraw[:, sl] * rq * gqb_ref[:, sl], qpar[:, sl] * rq * gqbs_ref[:, sl]
        kn2, kp = kraw[:, sl] * rk * gkb_ref[:, sl], kpar[:, sl] * rk * gkbs_ref[:, sl]
        for j in range(2):
            hh = 2 * c + j
            ls = slice(LANES * j, LANES * (j + 1))
            qb_o[0, hh, rows] = (qn[:, ls] * cb + qp[:, ls] * sb + qbias_ref[...]).astype(BF16)
            kb_o[0, hh, rows] = (kn2[:, ls] * cb + kp[:, ls] * sb + kbias_ref[...]).astype(BF16)

    vv = _dot(ckvn, w_v_ref[...])
    vt = vv.T
    row = lax.broadcasted_iota(jnp.int32, (VT_ROWS - V_HEAD, t), 0)
    tail = jnp.where(row == 0, 1.0, 0.0).astype(BF16)
    for hh in range(B_HEADS):
        vt_o[0, hh, sub, 0:V_HEAD, :] = vt[V_HEAD * hh:V_HEAD * (hh + 1), :].astype(BF16)
        vt_o[0, hh, sub, V_HEAD:VT_ROWS, :] = tail


def _pre_attention(x, mod, w, t):
    b, s, _ = x.shape
    nt = s // t
    const2 = lambda i, bb: (0, 0)
    tok2 = lambda i, bb: (i, 0)
    in_specs = [
        pl.BlockSpec((1, t, D_MODEL), lambda i, bb: (bb, i, 0)),
        pl.BlockSpec((1, 6, D_MODEL), lambda i, bb: (bb, 0, 0)),
        pl.BlockSpec((1, D_MODEL), const2),
        pl.BlockSpec((D_MODEL, C_END), const2),
        pl.BlockSpec((Q_LORA, B_PAD), const2),
        pl.BlockSpec((Q_LORA, B_PAD), const2),
        pl.BlockSpec((256, B_PAD), const2),
        pl.BlockSpec((256, B_PAD), const2),
        pl.BlockSpec((KV_LORA, B_HEADS * V_HEAD), const2),
        pl.BlockSpec((1, Q_LORA), const2),
        pl.BlockSpec((1, KV_LORA), const2),
        pl.BlockSpec((1, A_Q), const2),
        pl.BlockSpec((1, A_KV), const2),
        pl.BlockSpec((1, B_PAD), const2),
        pl.BlockSpec((1, B_PAD), const2),
        pl.BlockSpec((1, B_PAD), const2),
        pl.BlockSpec((1, B_PAD), const2),
        pl.BlockSpec((256, 256), const2),
        pl.BlockSpec((256, 256), const2),
        pl.BlockSpec((A_HEADS, 1, LANES), lambda i, bb: (0, 0, 0)),
        pl.BlockSpec((1, LANES), const2),
        pl.BlockSpec((1, LANES), const2),
        pl.BlockSpec((1, LANES), const2),
    ] + [pl.BlockSpec((t, LANES), tok2)] * 5
    out_shape = [
        jax.ShapeDtypeStruct((b, A_HEADS, s, LANES), BF16),
        jax.ShapeDtypeStruct((b, A_KV_HEADS, s, LANES), BF16),
        jax.ShapeDtypeStruct((b, A_KV_HEADS, s // WINDOW, VT_ROWS, WINDOW), BF16),
        jax.ShapeDtypeStruct((b, B_HEADS, s, LANES), BF16),
        jax.ShapeDtypeStruct((b, B_HEADS, s, LANES), BF16),
        jax.ShapeDtypeStruct((b, B_HEADS, s // PRE_SUB, VT_ROWS, PRE_SUB), BF16),
    ]
    out_specs = [
        pl.BlockSpec((1, A_HEADS, t, LANES), lambda i, bb: (bb, 0, i, 0)),
        pl.BlockSpec((1, A_KV_HEADS, t, LANES), lambda i, bb: (bb, 0, i, 0)),
        pl.BlockSpec((1, A_KV_HEADS, t // WINDOW, VT_ROWS, WINDOW), lambda i, bb: (bb, 0, i, 0, 0)),
        pl.BlockSpec((1, B_HEADS, t, LANES), lambda i, bb: (bb, 0, i, 0)),
        pl.BlockSpec((1, B_HEADS, t, LANES), lambda i, bb: (bb, 0, i, 0)),
        pl.BlockSpec((1, B_HEADS, t // PRE_SUB, VT_ROWS, PRE_SUB), lambda i, bb: (bb, 0, i, 0, 0)),
    ]
    return pl.pallas_call(
        _pre_kernel,
        grid=(nt, b),
        in_specs=in_specs,
        out_specs=out_specs,
        out_shape=out_shape,
        compiler_params=_cparams(2),
        name="pre_attention",
    )(x, mod, w["g_attn"], w["w_in"], w["w_uq"], w["w_uq_sw"], w["w_k"], w["w_k_sw"], w["w_v"], w["g_cq"], w["g_ckv"],
      w["gq_a"], w["gk_a"], w["gq_b"], w["gq_b_sw"], w["gk_b"], w["gk_b_sw"], w["bd64"], w["bd128"],
      w["qabias"], w["kabias"], w["qbias"], w["kbias"],
      w["ca"], w["sua"], w["sda"], w["cb"], w["sb"])


def _window_kernel(r, fast, sink_ref, q_ref, kp_ref, km_ref, kn_ref, vp_ref, vm_ref, vn_ref, o_ref,
                   kcat, bias_scr, o_scr):
    i = pl.program_id(1)
    last_i = pl.num_programs(1) - 1
    blk = WINDOW
    grp = A_HEADS // A_KV_HEADS
    nq = grp * blk

    @pl.when(i == 0)
    def _():
        kk = lax.broadcasted_iota(jnp.int32, (3 * blk, nq), 0)
        qq = lax.broadcasted_iota(jnp.int32, (3 * blk, nq), 1) & (blk - 1)
        base = jnp.where(kk >= qq, jnp.where(kk <= qq + 2 * blk, 0.0, NEG_INF), NEG_INF)
        bias_scr[0] = base
        bias_scr[1] = jnp.where(kk >= blk, base, NEG_INF)
        bias_scr[2] = jnp.where(kk < 2 * blk, base, NEG_INF)

    for hk in range(A_KV_HEADS):
        kcat[hk, 0:blk] = kp_ref[0, hk]
        kcat[hk, blk:(r + 1) * blk] = km_ref[0, hk]
        kcat[hk, (r + 1) * blk:(r + 2) * blk] = kn_ref[0, hk]
    units = [(j, hk) for j in range(r) for hk in range(A_KV_HEADS)]

    def scores(u):
        j, hk = units[u]
        q4 = jnp.concatenate([q_ref[0, hk * grp + g, j * blk:(j + 1) * blk, :] for g in range(grp)], axis=0)
        return _dot_nt(kcat[hk, j * blk:(j + 3) * blk, :], q4)

    pending = {0: scores(0)}
    for u, (j, hk) in enumerate(units):
        if u + 1 < len(units):
            pending[u + 1] = scores(u + 1)
        if j == 0:
            bias = bias_scr[jnp.where(i == 0, 1, 0)]
        elif j == r - 1:
            bias = bias_scr[jnp.where(i == last_i, 2, 0)]
        else:
            bias = bias_scr[0]
        st = pending.pop(u) + bias
        if fast:
            p = jnp.exp2(st).astype(BF16)
            sink_term = sink_ref[hk]
        else:
            srow = sink_ref[hk]
            m = jnp.maximum(jnp.max(st, axis=0, keepdims=True), srow)
            p = jnp.exp2(st - m).astype(BF16)
            sink_term = jnp.exp2(srow - m)
        vblk = [vp_ref[0, hk, 0]] + [vm_ref[0, hk, n] for n in range(r)] + [vn_ref[0, hk, 0]]
        acc = _dot(jnp.concatenate(vblk[j:j + 3], axis=1), p)
        ot = acc[0:A_HEAD_DIM] / (acc[A_HEAD_DIM:A_HEAD_DIM + 1] + sink_term)
        for g in range(grp):
            hq = hk * grp + g
            o_scr[hq * A_HEAD_DIM:(hq + 1) * A_HEAD_DIM, j * blk:(j + 1) * blk] = ot[:, g * blk:(g + 1) * blk]
    o_ref[0] = o_scr[...].T.astype(BF16)


def _window_attention(qa, ka, vat, sink_rows, tq, fast):
    b, _, s, _ = qa.shape
    r = tq // WINDOW
    nblk = s // WINDOW
    assert r >= 2
    grp = A_HEADS // A_KV_HEADS
    prev = lambda i: jnp.maximum(i * r - 1, 0)
    nxt = lambda i: jnp.minimum((i + 1) * r, nblk - 1)
    k_specs = [
        pl.BlockSpec((1, A_KV_HEADS, WINDOW, LANES), lambda bb, i: (bb, 0, prev(i), 0)),
        pl.BlockSpec((1, A_KV_HEADS, tq, LANES), lambda bb, i: (bb, 0, i, 0)),
        pl.BlockSpec((1, A_KV_HEADS, WINDOW, LANES), lambda bb, i: (bb, 0, nxt(i), 0)),
    ]
    v_specs = [
        pl.BlockSpec((1, A_KV_HEADS, 1, VT_ROWS, WINDOW), lambda bb, i: (bb, 0, prev(i), 0, 0)),
        pl.BlockSpec((1, A_KV_HEADS, r, VT_ROWS, WINDOW), lambda bb, i: (bb, 0, i, 0, 0)),
        pl.BlockSpec((1, A_KV_HEADS, 1, VT_ROWS, WINDOW), lambda bb, i: (bb, 0, nxt(i), 0, 0)),
    ]
    return pl.pallas_call(
        functools.partial(_window_kernel, r, fast),
        grid=(b, s // tq),
        in_specs=[pl.BlockSpec((A_KV_HEADS, 1, grp * WINDOW), lambda bb, i: (0, 0, 0)),
                  pl.BlockSpec((1, A_HEADS, tq, LANES), lambda bb, i: (bb, 0, i, 0))] + k_specs + v_specs,
        out_specs=pl.BlockSpec((1, tq, A_Q), lambda bb, i: (bb, i, 0)),
        out_shape=jax.ShapeDtypeStruct((b, s, A_Q), BF16),
        scratch_shapes=[pltpu.VMEM((A_KV_HEADS, tq + 2 * WINDOW, LANES), BF16),
                        pltpu.VMEM((3, 3 * WINDOW, grp * WINDOW), F32),
                        pltpu.VMEM((A_Q, tq), F32)],
        compiler_params=_cparams(2),
        name="window_attention_fast" if fast else "window_attention_safe",
    )(sink_rows, qa, ka, ka, ka, vat, vat, vat)


def _mla_safe_kernel(nc, q_ref, k_ref, vt_ref, o_ref, s_scr, o_scr):
    tk = vt_ref.shape[-1]
    tq = q_ref.shape[2]

    def head(h, carry):
        q = q_ref[0, h]

        def pass1(c, m):
            kc = k_ref[0, h, pl.ds(pl.multiple_of(c * tk, tk), tk), :]
            st = _dot_nt(kc, q)
            s_scr[c] = st
            return jnp.maximum(m, jnp.max(st, axis=0, keepdims=True))

        m = lax.fori_loop(0, nc, pass1, jnp.full((1, tq), -jnp.inf, F32))

        def pass2(c, acc):
            p = jnp.exp2(s_scr[c] - m).astype(BF16)
            return acc + _dot(vt_ref[0, h, c], p)

        acc = lax.fori_loop(0, nc, pass2, jnp.zeros((VT_ROWS, tq), F32))
        o_scr[pl.ds(pl.multiple_of(h * V_HEAD, V_HEAD), V_HEAD), :] = acc[0:V_HEAD] / acc[V_HEAD:V_HEAD + 1]
        return carry

    lax.fori_loop(0, B_HEADS, head, 0)
    o_ref[0] = o_scr[...].T.astype(BF16)


def _mla_fast_kernel(nc, q_ref, k_ref, vt_ref, o_ref, o_scr):
    tk = vt_ref.shape[-1]
    steps = [(h, c) for h in range(B_HEADS) for c in range(nc)]

    def scores(t):
        h, c = steps[t]
        return _dot_nt(k_ref[0, h, c * tk:(c + 1) * tk, :], q_ref[0, h])

    pending = {t: scores(t) for t in range(MLA_LOOKAHEAD)}
    acc = None
    for t, (h, c) in enumerate(steps):
        if t + MLA_LOOKAHEAD < len(steps):
            pending[t + MLA_LOOKAHEAD] = scores(t + MLA_LOOKAHEAD)
        p = jnp.exp2(pending.pop(t)).astype(BF16)
        d = _dot(vt_ref[0, h, c], p)
        acc = d if c == 0 else acc + d
        if c == nc - 1:
            o_scr[h * V_HEAD:(h + 1) * V_HEAD, :] = acc[0:V_HEAD] / acc[V_HEAD:V_HEAD + 1]
    o_ref[0] = o_scr[...].T.astype(BF16)


def _mla_attention(qb, kb, vt, tq, fast):
    b, _, s, _ = qb.shape
    nc, tk = vt.shape[2], vt.shape[4]
    scratch = [pltpu.VMEM((B_HEADS * V_HEAD, tq), F32)]
    if fast:
        body = functools.partial(_mla_fast_kernel, nc)
    else:
        body = functools.partial(_mla_safe_kernel, nc)
        scratch = [pltpu.VMEM((nc, tk, tq), F32)] + scratch
    return pl.pallas_call(
        body,
        grid=(b, s // tq),
        in_specs=[
            pl.BlockSpec((1, B_HEADS, tq, LANES), lambda bb, i: (bb, 0, i, 0)),
            pl.BlockSpec((1, B_HEADS, s, LANES), lambda bb, i: (bb, 0, 0, 0)),
            pl.BlockSpec((1, B_HEADS, nc, VT_ROWS, tk), lambda bb, i: (bb, 0, 0, 0, 0)),
        ],
        out_specs=pl.BlockSpec((1, tq, B_HEADS * V_HEAD), lambda bb, i: (bb, i, 0)),
        out_shape=jax.ShapeDtypeStruct((b, s, B_HEADS * V_HEAD), BF16),
        scratch_shapes=scratch,
        compiler_params=_cparams(2),
        name="mla_attention_fast" if fast else "mla_attention_safe",
    )(qb, kb, vt)


def _merge_kernel(x_ref, mod_ref, oa_ref, ob_ref, g1_ref, wg_ref, woa_ref, wob_ref, wout_ref, g_ref,
                  x1_o, h2_o):
    sh1 = mod_ref[0, 0:1, :]
    sc1 = mod_ref[0, 1:2, :]
    gt1 = mod_ref[0, 2:3, :]
    sh2 = mod_ref[0, 3:4, :]
    sc2 = mod_ref[0, 4:5, :]
    for sub in range(x_ref.shape[1] // PRE_SUB):
        rows = slice(sub * PRE_SUB, (sub + 1) * PRE_SUB)
        x = x_ref[0, rows]
        h = x * lax.rsqrt(jnp.mean(x * x, axis=-1, keepdims=True) + EPS) * g1_ref[...]
        hb = (h * (1.0 + sc1) + sh1).astype(BF16)
        mix = (jax.nn.sigmoid(_dot(hb, wg_ref[:, 0:D_MODEL])) * _dot(oa_ref[0, rows], woa_ref[...])
               + jax.nn.sigmoid(_dot(hb, wg_ref[:, D_MODEL:2 * D_MODEL])) * _dot(ob_ref[0, rows], wob_ref[...]))
        x1 = x + gt1 * _dot(mix.astype(BF16), wout_ref[...])
        x1_o[0, rows] = x1
        h2 = x1 * lax.rsqrt(jnp.mean(x1 * x1, axis=-1, keepdims=True) + EPS) * g_ref[...]
        h2_o[0, rows] = (h2 * (1.0 + sc2) + sh2).astype(BF16)


def _merge(x, mod, oa, ob, w, t):
    b, s, _ = x.shape
    tok = lambda i, bb: (bb, i, 0)
    const2 = lambda i, bb: (0, 0)
    return pl.pallas_call(
        _merge_kernel,
        grid=(s // t, b),
        in_specs=[
            pl.BlockSpec((1, t, D_MODEL), tok),
            pl.BlockSpec((1, 6, D_MODEL), lambda i, bb: (bb, 0, 0)),
            pl.BlockSpec((1, t, A_Q), tok),
            pl.BlockSpec((1, t, B_HEADS * V_HEAD), tok),
            pl.BlockSpec((1, D_MODEL), const2),
            pl.BlockSpec((D_MODEL, 2 * D_MODEL), const2),
            pl.BlockSpec((A_Q, D_MODEL), const2),
            pl.BlockSpec((B_HEADS * V_HEAD, D_MODEL), const2),
            pl.BlockSpec((D_MODEL, D_MODEL), const2),
            pl.BlockSpec((1, D_MODEL), const2),
        ],
        out_specs=[pl.BlockSpec((1, t, D_MODEL), tok), pl.BlockSpec((1, t, D_MODEL), tok)],
        out_shape=[jax.ShapeDtypeStruct((b, s, D_MODEL), F32), jax.ShapeDtypeStruct((b, s, D_MODEL), BF16)],
        compiler_params=_cparams(2),
        name="merge_outproj",
    )(x, mod, oa, ob, w["g_attn"], w["w_gate"], w["w_o_a"], w["w_o_b"], w["w_out"], w["g_ffn"])


MXU_DIM = 256
FF_SPLIT = (D_FF // MXU_DIM + 1) // 2 * MXU_DIM
FF_CHUNKS = ((0, FF_SPLIT), (FF_SPLIT, D_FF))
HALO = 16


def _ffn_kernel(x1_ref, mod_ref, hp_ref, hm_ref, hn_ref, wup_ref, cw_ref, cb_ref, wdn_ref, y_o,
                hcat, ua_scr, ug_scr):
    i = pl.program_id(0)
    last = pl.num_programs(0) - 1
    t = hm_ref.shape[1]
    gt2 = mod_ref[0, 5:6, :]
    hcat[0:HALO] = jnp.where(i > 0, hp_ref[0], jnp.zeros_like(hp_ref[0]))
    hcat[HALO:HALO + t] = hm_ref[0]
    hcat[HALO + t:HALO + t + HALO] = jnp.where(i < last, hn_ref[0], jnp.zeros_like(hn_ref[0]))
    hc = hcat[...]
    acc = None
    for c0, c1 in FF_CHUNKS:
        n = c1 - c0
        branch = []
        for off, scr in ((0, ua_scr), (D_FF, ug_scr)):
            lo = off + c0
            scr[:, 0:n] = _dot(hc, wup_ref[:, lo:lo + n])
            u = cb_ref[:, lo:lo + n]
            for j in range(3):
                u = u + scr[pl.ds(HALO - 1 + j, t), 0:n] * cw_ref[j:j + 1, lo:lo + n]
            branch.append(u)
        ua, ug = branch
        act = (ug * jax.nn.sigmoid(ug) * ua).astype(BF16)
        d = _dot(act, wdn_ref[c0:c1, :])
        acc = d if acc is None else acc + d
    y_o[0] = x1_ref[0] + gt2 * acc


def _ffn(x1, mod, h2, w, t):
    b, s, _ = x1.shape
    r = t // HALO
    nh = s // HALO
    tok = lambda i, bb: (bb, i, 0)
    const2 = lambda i, bb: (0, 0)
    return pl.pallas_call(
        _ffn_kernel,
        grid=(s // t, b),
        in_specs=[
            pl.BlockSpec((1, t, D_MODEL), tok),
            pl.BlockSpec((1, 6, D_MODEL), lambda i, bb: (bb, 0, 0)),
            pl.BlockSpec((1, HALO, D_MODEL), lambda i, bb: (bb, jnp.maximum(i * r - 1, 0), 0)),
            pl.BlockSpec((1, t, D_MODEL), tok),
            pl.BlockSpec((1, HALO, D_MODEL), lambda i, bb: (bb, jnp.minimum((i + 1) * r, nh - 1), 0)),
            pl.BlockSpec((D_MODEL, 2 * D_FF), const2, pipeline_mode=pl.Buffered(1)),
            pl.BlockSpec((3, 2 * D_FF), const2),
            pl.BlockSpec((1, 2 * D_FF), const2),
            pl.BlockSpec((D_FF, D_MODEL), const2, pipeline_mode=pl.Buffered(1)),
        ],
        out_specs=pl.BlockSpec((1, t, D_MODEL), tok),
        out_shape=jax.ShapeDtypeStruct((b, s, D_MODEL), F32),
        scratch_shapes=[
            pltpu.VMEM((t + 2 * HALO, D_MODEL), BF16),
            pltpu.VMEM((t + 2 * HALO, FF_SPLIT), F32),
            pltpu.VMEM((t + 2 * HALO, FF_SPLIT), F32),
        ],
        compiler_params=_cparams(2),
        name="conv_ffn",
    )(x1, mod, h2, h2, h2, w["w_up"], w["conv_w"], w["conv_b"], w["w_down"])


def _rope_tables(seq, dim, n_rep, lead, pad):
    half = dim // 2
    inv = 1.0 / (ROPE_THETA ** (jnp.arange(0, dim, 2, dtype=F32) / dim))
    ang = jnp.arange(seq, dtype=F32)[:, None] * inv[None, :]
    cos, sin = jnp.cos(ang), jnp.sin(ang)
    zero = jnp.zeros_like(sin)
    c = jnp.concatenate([jnp.ones((seq, lead), F32)] + [cos, cos] * n_rep + [jnp.zeros((seq, pad), F32)], axis=1)
    s_up = jnp.concatenate([jnp.zeros((seq, lead), F32)] + [-sin, zero] * n_rep + [jnp.zeros((seq, pad), F32)], axis=1)
    s_dn = jnp.concatenate([jnp.zeros((seq, lead), F32)] + [zero, sin] * n_rep + [jnp.zeros((seq, pad), F32)], axis=1)
    del half
    return c, s_up, s_dn


def _prepare(seq, w_ada, b_ada, g_attn, w_in, gq_a, gk_a, sink_a, g_cq, w_uq, g_ckv, w_ukv, gq_b, gk_b,
             w_o_a, w_o_b, w_out, g_ffn, w_up, conv_w, conv_b, w_down):
    kr_lo = A_Q + 2 * A_KV + Q_LORA + KV_LORA
    kr_hi = kr_lo + QK_ROPE
    w_in_p = jnp.concatenate([w_in[:, :kr_hi], jnp.zeros((D_MODEL, LANES - QK_ROPE), F32)], axis=1).astype(BF16)
    w_gate = w_in[:, kr_hi:].astype(BF16)
    pad_h = lambda a: jnp.pad(a, ((0, 0), (0, 0), (0, LANES - a.shape[2]))).reshape(a.shape[0], B_PAD)
    w_uq_p = pad_h(w_uq.reshape(Q_LORA, B_HEADS, B_QK)).astype(BF16)
    ukv = w_ukv.reshape(KV_LORA, B_HEADS, QK_NOPE + V_HEAD)
    wk_nope = pad_h(ukv[:, :, :QK_NOPE])
    eye = jnp.eye(QK_ROPE, dtype=F32)
    e_rope = jnp.pad(eye, ((0, 0), (QK_NOPE, LANES - B_QK)))
    wk_rope = jnp.tile(e_rope, (1, B_HEADS))
    w_k = jnp.concatenate([wk_nope, wk_rope, jnp.zeros((256 - KV_LORA - QK_ROPE, B_PAD), F32)], axis=0).astype(BF16)
    w_v = ukv[:, :, QK_NOPE:].reshape(KV_LORA, B_HEADS * V_HEAD).astype(BF16)
    pad_g = lambda g: jnp.tile(jnp.pad(g, (0, LANES - B_QK)), B_HEADS).reshape(1, B_PAD)
    idx = jnp.arange(256)
    bd = lambda n: (idx[:, None] // n == idx[None, :] // n).astype(BF16)
    ca, sua, sda = _rope_tables(seq, A_HEAD_DIM, 2, 0, 0)
    cb, sub, sdb = _rope_tables(seq, QK_ROPE, 1, QK_NOPE, LANES - B_QK)
    hl = jnp.arange(LANES)
    half = QK_ROPE // 2
    is_lo = (hl >= QK_NOPE) & (hl < QK_NOPE + half)
    is_hi = (hl >= QK_NOPE + half) & (hl < B_QK)
    partner = jnp.where(is_lo, hl + half, jnp.where(is_hi, hl - half, hl))
    has_partner = (is_lo | is_hi).astype(F32)

    def swap(a):
        a3 = a.reshape(a.shape[:-1] + (B_HEADS, LANES))
        return (jnp.take(a3, partner, axis=-1) * has_partner).astype(a.dtype).reshape(a.shape)
    bound = (B_QK ** 0.5) * LOG2E * jnp.max(jnp.abs(gq_b)) * jnp.max(jnp.abs(gk_b))
    shift = jnp.ceil(bound * 1.02) + 1.0
    lane = jnp.arange(LANES)
    qbias = jnp.where(lane == B_QK, 1.0, 0.0).astype(F32).reshape(1, LANES)
    kbias = jnp.where(lane == B_QK, -shift, 0.0).astype(F32).reshape(1, LANES)
    bound_a = (A_HEAD_DIM ** 0.5) * LOG2E * jnp.max(jnp.abs(gq_a)) * jnp.max(jnp.abs(gk_a))
    shift_a0 = jnp.ceil(bound_a * 1.02) + 1.0
    sink2 = sink_a * LOG2E
    shift_a = jnp.maximum(shift_a0, sink2).astype(BF16).astype(F32)
    qabias = jnp.where(lane[None, :] == A_HEAD_DIM, -shift_a[:, None], 0.0).reshape(A_HEADS, 1, LANES)
    kabias = jnp.where(lane == A_HEAD_DIM, 1.0, 0.0).astype(F32).reshape(1, LANES)
    rows = lambda v: jnp.repeat(v, WINDOW).reshape(A_KV_HEADS, 1, A_HEADS // A_KV_HEADS * WINDOW)
    return {
        "shift": shift, "qbias": qbias, "kbias": kbias,
        "shift_a": shift_a0, "qabias": qabias, "kabias": kabias,
        "sink_fast": rows(jnp.exp2(sink2 - shift_a)), "sink_safe": rows(sink2 - shift_a),
        "w_ada": w_ada, "b_ada": b_ada,
        "g_attn": g_attn.reshape(1, D_MODEL),
        "w_in": w_in_p, "w_gate": w_gate, "w_uq": w_uq_p, "w_k": w_k, "w_v": w_v,
        "w_uq_sw": swap(w_uq_p), "w_k_sw": swap(w_k),
        "gq_b_sw": swap(pad_g(gq_b)) * (B_QK ** -0.5 * LOG2E), "gk_b_sw": swap(pad_g(gk_b)), "sb": sub + sdb,
        "g_cq": g_cq.reshape(1, Q_LORA), "g_ckv": g_ckv.reshape(1, KV_LORA),
        "gq_a": (jnp.tile(gq_a, A_HEADS) * (A_HEAD_DIM ** -0.5 * LOG2E)).reshape(1, A_Q),
        "gk_a": jnp.tile(gk_a, A_KV_HEADS).reshape(1, A_KV),
        "gq_b": pad_g(gq_b) * (B_QK ** -0.5 * LOG2E), "gk_b": pad_g(gk_b),
        "bd64": bd(A_HEAD_DIM), "bd128": bd(LANES),
        "ca": ca, "sua": sua, "sda": sda, "cb": cb,
        "w_o_a": w_o_a.astype(BF16), "w_o_b": w_o_b.astype(BF16), "w_out": w_out.astype(BF16),
        "g_ffn": g_ffn.reshape(1, D_MODEL),
        "w_up": w_up.astype(BF16), "conv_w": conv_w, "conv_b": conv_b.reshape(1, 2 * D_FF),
        "w_down": w_down.astype(BF16),
    }


def _tile(s, pref):
    t = min(pref, s)
    assert s % t == 0 and t % WINDOW == 0, (s, t)
    return t


def _layer(x, c, w):
    b, s, _ = x.shape
    mod = _modulation(c, w["w_ada"], w["b_ada"]).reshape(b, 6, D_MODEL)
    qa, ka, va, qb, kb, vt = _pre_attention(x, mod, w, _tile(s, 512))
    tw = _tile(s, 512)
    oa = lax.cond(w["shift_a"] <= MAX_FIXED_SHIFT,
                  lambda *a: _window_attention(*a, w["sink_fast"], tw, True),
                  lambda *a: _window_attention(*a, w["sink_safe"], tw, False), qa, ka, va)
    tq = _tile(s, 256)
    ob = lax.cond(w["shift"] <= MAX_FIXED_SHIFT,
                  lambda *a: _mla_attention(*a, tq, True), lambda *a: _mla_attention(*a, tq, False), qb, kb, vt)
    x1, h2 = _merge(x, mod, oa, ob, w, _tile(s, 512))
    return _ffn(x1, mod, h2, w, _tile(s, 512))


def kernel(x_prompt, x_sample, c_prompt, c_sample, w_ada, b_ada, g_attn, w_in, gq_a, gk_a, sink_a, g_cq, w_uq,
           g_ckv, w_ukv, gq_b, gk_b, w_o_a, w_o_b, w_out, g_ffn, w_up, conv_w, conv_b, w_down):
    y_prompt, y_sample = x_prompt, x_sample
    for l in range(w_ada.shape[0]):
        p = (w_ada[l], b_ada[l], g_attn[l], w_in[l], gq_a[l], gk_a[l], sink_a[l], g_cq[l], w_uq[l], g_ckv[l],
             w_ukv[l], gq_b[l], gk_b[l], w_o_a[l], w_o_b[l], w_out[l], g_ffn[l], w_up[l], conv_w[l], conv_b[l],
             w_down[l])
        assert y_prompt.shape[1] == y_sample.shape[1]
        w = _prepare(y_prompt.shape[1], *p)
        y_prompt = _layer(y_prompt, c_prompt, w)
        y_sample = _layer(y_sample, c_sample, w)
    return (y_prompt, y_sample)
```

```python
import functools

import jax
import jax.numpy as jnp
from jax import lax
from jax.experimental import pallas as pl
from jax.experimental.pallas import tpu as pltpu

F32 = jnp.float32
BF16 = jnp.bfloat16

D_MODEL = 1024
A_HEADS = 8
A_KV_HEADS = 2
A_HEAD_DIM = 64
WINDOW = 128
B_HEADS = 8
Q_LORA = 256
KV_LORA = 128
QK_NOPE = 64
QK_ROPE = 32
V_HEAD = 64
D_FF = 2816
ROPE_THETA = 10000.0
EPS = 1e-6
NEG_INF = -1e30

A_Q = A_HEADS * A_HEAD_DIM
A_KV = A_KV_HEADS * A_HEAD_DIM
B_QK = QK_NOPE + QK_ROPE
LANES = 128
B_PAD = B_HEADS * LANES
VT_ROWS = 80

C_QA, C_KA, C_VA, C_CQ, C_CKV, C_KR, C_END = 0, 512, 640, 768, 1024, 1152, 1280

VMEM_LIMIT = 56 * 1024 * 1024
LOG2E = 1.4426950408889634
MAX_FIXED_SHIFT = 50.0
MLA_LOOKAHEAD = 4
PRE_SUB = 256


def _cparams(n_axes):
    return pltpu.CompilerParams(dimension_semantics=("arbitrary",) * n_axes, vmem_limit_bytes=VMEM_LIMIT)


def _dot(a, b):
    return jnp.dot(a, b, preferred_element_type=F32)


def _dot_nt(a, b):
    return lax.dot_general(a, b, (((1,), (1,)), ((), ())), preferred_element_type=F32)


def _mod_kernel(c_ref, w_ref, b_ref, o_ref):
    c = c_ref[...]
    sc = c * jax.nn.sigmoid(c)
    o_ref[...] = _dot(sc.astype(BF16), w_ref[...].astype(BF16)) + b_ref[...]


def _modulation(c, w_ada, b_ada):
    b = c.shape[0]
    n = w_ada.shape[1]
    tn = 1024
    return pl.pallas_call(
        _mod_kernel,
        grid=(n // tn,),
        in_specs=[
            pl.BlockSpec((b, D_MODEL), lambda j: (0, 0)),
            pl.BlockSpec((D_MODEL, tn), lambda j: (0, j)),
            pl.BlockSpec((1, tn), lambda j: (0, j)),
        ],
        out_specs=pl.BlockSpec((b, tn), lambda j: (0, j)),
        out_shape=jax.ShapeDtypeStruct((b, n), F32),
        compiler_params=_cparams(1),
        name="adaln_mod",
    )(c, w_ada, b_ada.reshape(1, n))


def _group_norm(x, bd, inv_n, gain):
    ss = _dot((x * x).astype(BF16), bd)
    return x * lax.rsqrt(ss * inv_n + EPS) * gain


def _rope(x, c, s_up, s_dn, shift):
    return x * c + pltpu.roll(x, LANES - shift, 1) * s_up + pltpu.roll(x, shift, 1) * s_dn


def _pre_kernel(*refs):
    n_sub = refs[0].shape[1] // PRE_SUB
    z = _pre_project(0, *refs)
    for sub in range(n_sub):
        z_next = _pre_project(sub + 1, *refs) if sub + 1 < n_sub else None
        _pre_finish(sub, z, *refs)
        z = z_next


def _pre_project(sub, x_ref, mod_ref, g_ref, w_in_ref, *_):
    x = x_ref[0, sub * PRE_SUB:(sub + 1) * PRE_SUB]
    sh1 = mod_ref[0, 0:1, :]
    sc1 = mod_ref[0, 1:2, :]
    h = x * lax.rsqrt(jnp.mean(x * x, axis=-1, keepdims=True) + EPS) * g_ref[...]
    hb = (h * (1.0 + sc1) + sh1).astype(BF16)
    return _dot(hb, w_in_ref[...])


def _pre_finish(sub, z, x_ref, mod_ref, g_ref, w_in_ref, w_uq_ref, w_uqs_ref, w_k_ref, w_ks_ref, w_v_ref, gcq_ref,
                gckv_ref, gqa_ref, gka_ref, gqb_ref, gqbs_ref, gkb_ref, gkbs_ref, bd64_ref, bd128_ref,
                qabias_ref, kabias_ref, qbias_ref, kbias_ref, ca_ref, sua_ref, sda_ref, cb_ref, sb_ref,
                qa_o, ka_o, va_o, qb_o, kb_o, vt_o):
    t = PRE_SUB
    rows = slice(sub * t, (sub + 1) * t)
    ca, sua, sda = ca_ref[rows], sua_ref[rows], sda_ref[rows]
    cb, sb = cb_ref[rows], sb_ref[rows]
    bd64 = bd64_ref[...]
    bd128 = bd128_ref[...]

    low_half = lax.broadcasted_iota(jnp.int32, (t, LANES), 1) < A_HEAD_DIM
    row = lax.broadcasted_iota(jnp.int32, (VT_ROWS - V_HEAD, LANES), 0)
    tail_a = jnp.where(row == 0, 1.0, 0.0).astype(BF16)

    def split_heads(pair):
        return jnp.where(low_half, pair, 0.0), jnp.where(low_half, pltpu.roll(pair, A_HEAD_DIM, 1), 0.0)

    zq = z[:, C_QA:C_KA]
    for c in range(2):
        xn = _group_norm(zq[:, 256 * c:256 * (c + 1)], bd64, 1.0 / A_HEAD_DIM, gqa_ref[:, 256 * c:256 * (c + 1)])
        for j in range(2):
            hq = 4 * c + 2 * j
            even, odd = split_heads(_rope(xn[:, LANES * j:LANES * (j + 1)], ca, sua, sda, 32))
            qa_o[0, hq, rows] = (even + qabias_ref[hq]).astype(BF16)
            qa_o[0, hq + 1, rows] = (odd + qabias_ref[hq + 1]).astype(BF16)
    zk = z[:, C_KA:C_VA]
    ssk = _dot((zk * zk).astype(BF16), bd64[0:LANES, 0:LANES])
    kn = zk * lax.rsqrt(ssk * (1.0 / A_HEAD_DIM) + EPS) * gka_ref[...]
    for hk, kh in enumerate(split_heads(_rope(kn, ca, sua, sda, 32))):
        ka_o[0, hk, rows] = (kh + kabias_ref[...]).astype(BF16)
    vat = z[:, C_VA:C_CQ].T
    for hk in range(A_KV_HEADS):
        for blk in range(t // WINDOW):
            ob = sub * (t // WINDOW) + blk
            va_o[0, hk, ob, 0:A_HEAD_DIM, :] = vat[A_HEAD_DIM * hk:A_HEAD_DIM * (hk + 1),
                                                   WINDOW * blk:WINDOW * (blk + 1)].astype(BF16)
            va_o[0, hk, ob, A_HEAD_DIM:VT_ROWS, :] = tail_a

    cq = z[:, C_CQ:C_CKV]
    cqn = (cq * lax.rsqrt(jnp.mean(cq * cq, axis=-1, keepdims=True) + EPS) * gcq_ref[...]).astype(BF16)
    ckv = z[:, C_CKV:C_KR]
    ckvn = (ckv * lax.rsqrt(jnp.mean(ckv * ckv, axis=-1, keepdims=True) + EPS) * gckv_ref[...]).astype(BF16)
    krp = z[:, C_KR:C_END].astype(BF16)

    k_lhs = jnp.concatenate([ckvn, krp], axis=1)
    qraw, qpar = _dot(cqn, w_uq_ref[...]), _dot(cqn, w_uqs_ref[...])
    kraw, kpar = _dot(k_lhs, w_k_ref[...]), _dot(k_lhs, w_ks_ref[...])
    for c in range(4):
        sl = slice(256 * c, 256 * (c + 1))
        rq = lax.rsqrt(_dot((qraw[:, sl] * qraw[:, sl]).astype(BF16), bd128) * (1.0 / B_QK) + EPS)
        rk = lax.rsqrt(_dot((kraw[:, sl] * kraw[:, sl]).astype(BF16), bd128) * (1.0 / B_QK) + EPS)
        qn, qp = qraw[:, sl] * rq * gqb_ref[:, sl], qpar[:, sl] * rq * gqbs_ref[:, sl]
        kn2, kp = kraw[:, sl] * rk * gkb_ref[:, sl], kpar[:, sl] * rk * gkbs_ref[:, sl]
        for j in range(2):
            hh = 2 * c + j
            ls = slice(LANES * j, LANES * (j + 1))
            qb_o[0, hh, rows] = (qn[:, ls] * cb + qp[:, ls] * sb + qbias_ref[...]).astype(BF16)
            kb_o[0, hh, rows] = (kn2[:, ls] * cb + kp[:, ls] * sb + kbias_ref[...]).astype(BF16)

    vv = _dot(ckvn, w_v_ref[...])
    vt = vv.T
    row = lax.broadcasted_iota(jnp.int32, (VT_ROWS - V_HEAD, t), 0)
    tail = jnp.where(row == 0, 1.0, 0.0).astype(BF16)
    for hh in range(B_HEADS):
        vt_o[0, hh, sub, 0:V_HEAD, :] = vt[V_HEAD * hh:V_HEAD * (hh + 1), :].astype(BF16)
        vt_o[0, hh, sub, V_HEAD:VT_ROWS, :] = tail


def _pre_attention(x, mod, w, t):
    b, s, _ = x.shape
    nt = s // t
    const2 = lambda i, bb: (0, 0)
    tok2 = lambda i, bb: (i, 0)
    in_specs = [
        pl.BlockSpec((1, t, D_MODEL), lambda i, bb: (bb, i, 0)),
        pl.BlockSpec((1, 6, D_MODEL), lambda i, bb: (bb, 0, 0)),
        pl.BlockSpec((1, D_MODEL), const2),
        pl.BlockSpec((D_MODEL, C_END), const2),
        pl.BlockSpec((Q_LORA, B_PAD), const2),
        pl.BlockSpec((Q_LORA, B_PAD), const2),
        pl.BlockSpec((256, B_PAD), const2),
        pl.BlockSpec((256, B_PAD), const2),
        pl.BlockSpec((KV_LORA, B_HEADS * V_HEAD), const2),
        pl.BlockSpec((1, Q_LORA), const2),
        pl.BlockSpec((1, KV_LORA), const2),
        pl.BlockSpec((1, A_Q), const2),
        pl.BlockSpec((1, A_KV), const2),
        pl.BlockSpec((1, B_PAD), const2),
        pl.BlockSpec((1, B_PAD), const2),
        pl.BlockSpec((1, B_PAD), const2),
        pl.BlockSpec((1, B_PAD), const2),
        pl.BlockSpec((256, 256), const2),
        pl.BlockSpec((256, 256), const2),
        pl.BlockSpec((A_HEADS, 1, LANES), lambda i, bb: (0, 0, 0)),
        pl.BlockSpec((1, LANES), const2),
        pl.BlockSpec((1, LANES), const2),
        pl.BlockSpec((1, LANES), const2),
    ] + [pl.BlockSpec((t, LANES), tok2)] * 5
    out_shape = [
        jax.ShapeDtypeStruct((b, A_HEADS, s, LANES), BF16),
        jax.ShapeDtypeStruct((b, A_KV_HEADS, s, LANES), BF16),
        jax.ShapeDtypeStruct((b, A_KV_HEADS, s // WINDOW, VT_ROWS, WINDOW), BF16),
        jax.ShapeDtypeStruct((b, B_HEADS, s, LANES), BF16),
        jax.ShapeDtypeStruct((b, B_HEADS, s, LANES), BF16),
        jax.ShapeDtypeStruct((b, B_HEADS, s // PRE_SUB, VT_ROWS, PRE_SUB), BF16),
    ]
    out_specs = [
        pl.BlockSpec((1, A_HEADS, t, LANES), lambda i, bb: (bb, 0, i, 0)),
        pl.BlockSpec((1, A_KV_HEADS, t, LANES), lambda i, bb: (bb, 0, i, 0)),
        pl.BlockSpec((1, A_KV_HEADS, t // WINDOW, VT_ROWS, WINDOW), lambda i, bb: (bb, 0, i, 0, 0)),
        pl.BlockSpec((1, B_HEADS, t, LANES), lambda i, bb: (bb, 0, i, 0)),
        pl.BlockSpec((1, B_HEADS, t, LANES), lambda i, bb: (bb, 0, i, 0)),
        pl.BlockSpec((1, B_HEADS, t // PRE_SUB, VT_ROWS, PRE_SUB), lambda i, bb: (bb, 0, i, 0, 0)),
    ]
    return pl.pallas_call(
        _pre_kernel,
        grid=(nt, b),
        in_specs=in_specs,
        out_specs=out_specs,
        out_shape=out_shape,
        compiler_params=_cparams(2),
        name="pre_attention",
    )(x, mod, w["g_attn"], w["w_in"], w["w_uq"], w["w_uq_sw"], w["w_k"], w["w_k_sw"], w["w_v"], w["g_cq"], w["g_ckv"],
      w["gq_a"], w["gk_a"], w["gq_b"], w["gq_b_sw"], w["gk_b"], w["gk_b_sw"], w["bd64"], w["bd128"],
      w["qabias"], w["kabias"], w["qbias"], w["kbias"],
      w["ca"], w["sua"], w["sda"], w["cb"], w["sb"])


def _window_kernel(r, fast, sink_ref, q_ref, kp_ref, km_ref, kn_ref, vp_ref, vm_ref, vn_ref, o_ref,
                   kcat, bias_scr, o_scr):
    i = pl.program_id(1)
    last_i = pl.num_programs(1) - 1
    blk = WINDOW
    grp = A_HEADS // A_KV_HEADS
    nq = grp * blk

    @pl.when(i == 0)
    def _():
        kk = lax.broadcasted_iota(jnp.int32, (3 * blk, nq), 0)
        qq = lax.broadcasted_iota(jnp.int32, (3 * blk, nq), 1) & (blk - 1)
        base = jnp.where(kk >= qq, jnp.where(kk <= qq + 2 * blk, 0.0, NEG_INF), NEG_INF)
        bias_scr[0] = base
        bias_scr[1] = jnp.where(kk >= blk, base, NEG_INF)
        bias_scr[2] = jnp.where(kk < 2 * blk, base, NEG_INF)

    for hk in range(A_KV_HEADS):
        kcat[hk, 0:blk] = kp_ref[0, hk]
        kcat[hk, blk:(r + 1) * blk] = km_ref[0, hk]
        kcat[hk, (r + 1) * blk:(r + 2) * blk] = kn_ref[0, hk]
    units = [(j, hk) for j in range(r) for hk in range(A_KV_HEADS)]

    def scores(u):
        j, hk = units[u]
        q4 = jnp.concatenate([q_ref[0, hk * grp + g, j * blk:(j + 1) * blk, :] for g in range(grp)], axis=0)
        return _dot_nt(kcat[hk, j * blk:(j + 3) * blk, :], q4)

    pending = {0: scores(0)}
    for u, (j, hk) in enumerate(units):
        if u + 1 < len(units):
            pending[u + 1] = scores(u + 1)
        if j == 0:
            bias = bias_scr[jnp.where(i == 0, 1, 0)]
        elif j == r - 1:
            bias = bias_scr[jnp.where(i == last_i, 2, 0)]
        else:
            bias = bias_scr[0]
        st = pending.pop(u) + bias
        if fast:
            p = jnp.exp2(st).astype(BF16)
            sink_term = sink_ref[hk]
        else:
            srow = sink_ref[hk]
            m = jnp.maximum(jnp.max(st, axis=0, keepdims=True), srow)
            p = jnp.exp2(st - m).astype(BF16)
            sink_term = jnp.exp2(srow - m)
        vblk = [vp_ref[0, hk, 0]] + [vm_ref[0, hk, n] for n in range(r)] + [vn_ref[0, hk, 0]]
        acc = _dot(jnp.concatenate(vblk[j:j + 3], axis=1), p)
        ot = acc[0:A_HEAD_DIM] / (acc[A_HEAD_DIM:A_HEAD_DIM + 1] + sink_term)
        for g in range(grp):
            hq = hk * grp + g
            o_scr[hq * A_HEAD_DIM:(hq + 1) * A_HEAD_DIM, j * blk:(j + 1) * blk] = ot[:, g * blk:(g + 1) * blk]
    o_ref[0] = o_scr[...].T.astype(BF16)


def _window_attention(qa, ka, vat, sink_rows, tq, fast):
    b, _, s, _ = qa.shape
    r = tq // WINDOW
    nblk = s // WINDOW
    assert r >= 2
    grp = A_HEADS // A_KV_HEADS
    prev = lambda i: jnp.maximum(i * r - 1, 0)
    nxt = lambda i: jnp.minimum((i + 1) * r, nblk - 1)
    k_specs = [
        pl.BlockSpec((1, A_KV_HEADS, WINDOW, LANES), lambda bb, i: (bb, 0, prev(i), 0)),
        pl.BlockSpec((1, A_KV_HEADS, tq, LANES), lambda bb, i: (bb, 0, i, 0)),
        pl.BlockSpec((1, A_KV_HEADS, WINDOW, LANES), lambda bb, i: (bb, 0, nxt(i), 0)),
    ]
    v_specs = [
        pl.BlockSpec((1, A_KV_HEADS, 1, VT_ROWS, WINDOW), lambda bb, i: (bb, 0, prev(i), 0, 0)),
        pl.BlockSpec((1, A_KV_HEADS, r, VT_ROWS, WINDOW), lambda bb, i: (bb, 0, i, 0, 0)),
        pl.BlockSpec((1, A_KV_HEADS, 1, VT_ROWS, WINDOW), lambda bb, i: (bb, 0, nxt(i), 0, 0)),
    ]
    return pl.pallas_call(
        functools.partial(_window_kernel, r, fast),
        grid=(b, s // tq),
        in_specs=[pl.BlockSpec((A_KV_HEADS, 1, grp * WINDOW), lambda bb, i: (0, 0, 0)),
                  pl.BlockSpec((1, A_HEADS, tq, LANES), lambda bb, i: (bb, 0, i, 0))] + k_specs + v_specs,
        out_specs=pl.BlockSpec((1, tq, A_Q), lambda bb, i: (bb, i, 0)),
        out_shape=jax.ShapeDtypeStruct((b, s, A_Q), BF16),
        scratch_shapes=[pltpu.VMEM((A_KV_HEADS, tq + 2 * WINDOW, LANES), BF16),
                        pltpu.VMEM((3, 3 * WINDOW, grp * WINDOW), F32),
                        pltpu.VMEM((A_Q, tq), F32)],
        compiler_params=_cparams(2),
        name="window_attention_fast" if fast else "window_attention_safe",
    )(sink_rows, qa, ka, ka, ka, vat, vat, vat)


def _mla_safe_kernel(nc, q_ref, k_ref, vt_ref, o_ref, s_scr, o_scr):
    tk = vt_ref.shape[-1]
    tq = q_ref.shape[2]

    def head(h, carry):
        q = q_ref[0, h]

        def pass1(c, m):
            kc = k_ref[0, h, pl.ds(pl.multiple_of(c * tk, tk), tk), :]
            st = _dot_nt(kc, q)
            s_scr[c] = st
            return jnp.maximum(m, jnp.max(st, axis=0, keepdims=True))

        m = lax.fori_loop(0, nc, pass1, jnp.full((1, tq), -jnp.inf, F32))

        def pass2(c, acc):
            p = jnp.exp2(s_scr[c] - m).astype(BF16)
            return acc + _dot(vt_ref[0, h, c], p)

        acc = lax.fori_loop(0, nc, pass2, jnp.zeros((VT_ROWS, tq), F32))
        o_scr[pl.ds(pl.multiple_of(h * V_HEAD, V_HEAD), V_HEAD), :] = acc[0:V_HEAD] / acc[V_HEAD:V_HEAD + 1]
        return carry

    lax.fori_loop(0, B_HEADS, head, 0)
    o_ref[0] = o_scr[...].T.astype(BF16)


def _mla_fast_kernel(nc, q_ref, k_ref, vt_ref, o_ref, o_scr):
    tk = vt_ref.shape[-1]
    steps = [(h, c) for h in range(B_HEADS) for c in range(nc)]

    def scores(t):
        h, c = steps[t]
        return _dot_nt(k_ref[0, h, c * tk:(c + 1) * tk, :], q_ref[0, h])

    pending = {t: scores(t) for t in range(MLA_LOOKAHEAD)}
    acc = None
    for t, (h, c) in enumerate(steps):
        if t + MLA_LOOKAHEAD < len(steps):
            pending[t + MLA_LOOKAHEAD] = scores(t + MLA_LOOKAHEAD)
        p = jnp.exp2(pending.pop(t)).astype(BF16)
        d = _dot(vt_ref[0, h, c], p)
        acc = d if c == 0 else acc + d
        if c == nc - 1:
            o_scr[h * V_HEAD:(h + 1) * V_HEAD, :] = acc[0:V_HEAD] / acc[V_HEAD:V_HEAD + 1]
    o_ref[0] = o_scr[...].T.astype(BF16)


def _mla_attention(qb, kb, vt, tq, fast):
    b, _, s, _ = qb.shape
    nc, tk = vt.shape[2], vt.shape[4]
    scratch = [pltpu.VMEM((B_HEADS * V_HEAD, tq), F32)]
    if fast:
        body = functools.partial(_mla_fast_kernel, nc)
    else:
        body = functools.partial(_mla_safe_kernel, nc)
        scratch = [pltpu.VMEM((nc, tk, tq), F32)] + scratch
    return pl.pallas_call(
        body,
        grid=(b, s // tq),
        in_specs=[
            pl.BlockSpec((1, B_HEADS, tq, LANES), lambda bb, i: (bb, 0, i, 0)),
            pl.BlockSpec((1, B_HEADS, s, LANES), lambda bb, i: (bb, 0, 0, 0)),
            pl.BlockSpec((1, B_HEADS, nc, VT_ROWS, tk), lambda bb, i: (bb, 0, 0, 0, 0)),
        ],
        out_specs=pl.BlockSpec((1, tq, B_HEADS * V_HEAD), lambda bb, i: (bb, i, 0)),
        out_shape=jax.ShapeDtypeStruct((b, s, B_HEADS * V_HEAD), BF16),
        scratch_shapes=scratch,
        compiler_params=_cparams(2),
        name="mla_attention_fast" if fast else "mla_attention_safe",
    )(qb, kb, vt)


def _merge_kernel(x_ref, mod_ref, oa_ref, ob_ref, g1_ref, wg_ref, woa_ref, wob_ref, wout_ref, g_ref,
                  x1_o, h2_o):
    sh1 = mod_ref[0, 0:1, :]
    sc1 = mod_ref[0, 1:2, :]
    gt1 = mod_ref[0, 2:3, :]
    sh2 = mod_ref[0, 3:4, :]
    sc2 = mod_ref[0, 4:5, :]
    n_sub = x_ref.shape[1] // PRE_SUB

    def gated_mix(sub):
        rows = slice(sub * PRE_SUB, (sub + 1) * PRE_SUB)
        x = x_ref[0, rows]
        h = x * lax.rsqrt(jnp.mean(x * x, axis=-1, keepdims=True) + EPS) * g1_ref[...]
        hb = (h * (1.0 + sc1) + sh1).astype(BF16)
        mix = (jax.nn.sigmoid(_dot(hb, wg_ref[:, 0:D_MODEL])) * _dot(oa_ref[0, rows], woa_ref[...])
               + jax.nn.sigmoid(_dot(hb, wg_ref[:, D_MODEL:2 * D_MODEL])) * _dot(ob_ref[0, rows], wob_ref[...]))
        return x, mix.astype(BF16)

    cur = gated_mix(0)
    for sub in range(n_sub):
        nxt = gated_mix(sub + 1) if sub + 1 < n_sub else None
        rows = slice(sub * PRE_SUB, (sub + 1) * PRE_SUB)
        x, mix = cur
        x1 = x + gt1 * _dot(mix, wout_ref[...])
        x1_o[0, rows] = x1
        h2 = x1 * lax.rsqrt(jnp.mean(x1 * x1, axis=-1, keepdims=True) + EPS) * g_ref[...]
        h2_o[0, rows] = (h2 * (1.0 + sc2) + sh2).astype(BF16)
        cur = nxt


def _merge(x, mod, oa, ob, w, t):
    b, s, _ = x.shape
    tok = lambda i, bb: (bb, i, 0)
    const2 = lambda i, bb: (0, 0)
    return pl.pallas_call(
        _merge_kernel,
        grid=(s // t, b),
        in_specs=[
            pl.BlockSpec((1, t, D_MODEL), tok),
            pl.BlockSpec((1, 6, D_MODEL), lambda i, bb: (bb, 0, 0)),
            pl.BlockSpec((1, t, A_Q), tok),
            pl.BlockSpec((1, t, B_HEADS * V_HEAD), tok),
            pl.BlockSpec((1, D_MODEL), const2),
            pl.BlockSpec((D_MODEL, 2 * D_MODEL), const2),
            pl.BlockSpec((A_Q, D_MODEL), const2),
            pl.BlockSpec((B_HEADS * V_HEAD, D_MODEL), const2),
            pl.BlockSpec((D_MODEL, D_MODEL), const2),
            pl.BlockSpec((1, D_MODEL), const2),
        ],
        out_specs=[pl.BlockSpec((1, t, D_MODEL), tok), pl.BlockSpec((1, t, D_MODEL), tok)],
        out_shape=[jax.ShapeDtypeStruct((b, s, D_MODEL), F32), jax.ShapeDtypeStruct((b, s, D_MODEL), BF16)],
        compiler_params=_cparams(2),
        name="merge_outproj",
    )(x, mod, oa, ob, w["g_attn"], w["w_gate"], w["w_o_a"], w["w_o_b"], w["w_out"], w["g_ffn"])


MXU_DIM = 256
FF_SPLIT = (D_FF // MXU_DIM + 1) // 2 * MXU_DIM
FF_CHUNKS = ((0, FF_SPLIT), (FF_SPLIT, D_FF))
HALO = 16


def _ffn_kernel(x1_ref, mod_ref, hp_ref, hm_ref, hn_ref, wup_ref, cw_ref, cb_ref, wdn_ref, y_o,
                hcat, ua_scr, ug_scr):
    i = pl.program_id(0)
    last = pl.num_programs(0) - 1
    t = hm_ref.shape[1]
    gt2 = mod_ref[0, 5:6, :]
    hcat[0:HALO] = jnp.where(i > 0, hp_ref[0], jnp.zeros_like(hp_ref[0]))
    hcat[HALO:HALO + t] = hm_ref[0]
    hcat[HALO + t:HALO + t + HALO] = jnp.where(i < last, hn_ref[0], jnp.zeros_like(hn_ref[0]))
    hc = hcat[...]
    acc = None
    for c0, c1 in FF_CHUNKS:
        n = c1 - c0
        branch = []
        for off, scr in ((0, ua_scr), (D_FF, ug_scr)):
            lo = off + c0
            scr[:, 0:n] = _dot(hc, wup_ref[:, lo:lo + n])
            u = cb_ref[:, lo:lo + n]
            for j in range(3):
                u = u + scr[pl.ds(HALO - 1 + j, t), 0:n] * cw_ref[j:j + 1, lo:lo + n]
            branch.append(u)
        ua, ug = branch
        act = (ug * jax.nn.sigmoid(ug) * ua).astype(BF16)
        d = _dot(act, wdn_ref[c0:c1, :])
        acc = d if acc is None else acc + d
    y_o[0] = x1_ref[0] + gt2 * acc


def _ffn(x1, mod, h2, w, t):
    b, s, _ = x1.shape
    r = t // HALO
    nh = s // HALO
    tok = lambda i, bb: (bb, i, 0)
    const2 = lambda i, bb: (0, 0)
    return pl.pallas_call(
        _ffn_kernel,
        grid=(s // t, b),
        in_specs=[
            pl.BlockSpec((1, t, D_MODEL), tok),
            pl.BlockSpec((1, 6, D_MODEL), lambda i, bb: (bb, 0, 0)),
            pl.BlockSpec((1, HALO, D_MODEL), lambda i, bb: (bb, jnp.maximum(i * r - 1, 0), 0)),
            pl.BlockSpec((1, t, D_MODEL), tok),
            pl.BlockSpec((1, HALO, D_MODEL), lambda i, bb: (bb, jnp.minimum((i + 1) * r, nh - 1), 0)),
            pl.BlockSpec((D_MODEL, 2 * D_FF), const2, pipeline_mode=pl.Buffered(1)),
            pl.BlockSpec((3, 2 * D_FF), const2),
            pl.BlockSpec((1, 2 * D_FF), const2),
            pl.BlockSpec((D_FF, D_MODEL), const2, pipeline_mode=pl.Buffered(1)),
        ],
        out_specs=pl.BlockSpec((1, t, D_MODEL), tok),
        out_shape=jax.ShapeDtypeStruct((b, s, D_MODEL), F32),
        scratch_shapes=[
            pltpu.VMEM((t + 2 * HALO, D_MODEL), BF16),
            pltpu.VMEM((t + 2 * HALO, FF_SPLIT), F32),
            pltpu.VMEM((t + 2 * HALO, FF_SPLIT), F32),
        ],
        compiler_params=_cparams(2),
        name="conv_ffn",
    )(x1, mod, h2, h2, h2, w["w_up"], w["conv_w"], w["conv_b"], w["w_down"])


def _rope_tables(seq, dim, n_rep, lead, pad):
    half = dim // 2
    inv = 1.0 / (ROPE_THETA ** (jnp.arange(0, dim, 2, dtype=F32) / dim))
    ang = jnp.arange(seq, dtype=F32)[:, None] * inv[None, :]
    cos, sin = jnp.cos(ang), jnp.sin(ang)
    zero = jnp.zeros_like(sin)
    c = jnp.concatenate([jnp.ones((seq, lead), F32)] + [cos, cos] * n_rep + [jnp.zeros((seq, pad), F32)], axis=1)
    s_up = jnp.concatenate([jnp.zeros((seq, lead), F32)] + [-sin, zero] * n_rep + [jnp.zeros((seq, pad), F32)], axis=1)
    s_dn = jnp.concatenate([jnp.zeros((seq, lead), F32)] + [zero, sin] * n_rep + [jnp.zeros((seq, pad), F32)], axis=1)
    del half
    return c, s_up, s_dn


def _prepare(seq, w_ada, b_ada, g_attn, w_in, gq_a, gk_a, sink_a, g_cq, w_uq, g_ckv, w_ukv, gq_b, gk_b,
             w_o_a, w_o_b, w_out, g_ffn, w_up, conv_w, conv_b, w_down):
    kr_lo = A_Q + 2 * A_KV + Q_LORA + KV_LORA
    kr_hi = kr_lo + QK_ROPE
    w_in_p = jnp.concatenate([w_in[:, :kr_hi], jnp.zeros((D_MODEL, LANES - QK_ROPE), F32)], axis=1).astype(BF16)
    w_gate = w_in[:, kr_hi:].astype(BF16)
    pad_h = lambda a: jnp.pad(a, ((0, 0), (0, 0), (0, LANES - a.shape[2]))).reshape(a.shape[0], B_PAD)
    w_uq_p = pad_h(w_uq.reshape(Q_LORA, B_HEADS, B_QK)).astype(BF16)
    ukv = w_ukv.reshape(KV_LORA, B_HEADS, QK_NOPE + V_HEAD)
    wk_nope = pad_h(ukv[:, :, :QK_NOPE])
    eye = jnp.eye(QK_ROPE, dtype=F32)
    e_rope = jnp.pad(eye, ((0, 0), (QK_NOPE, LANES - B_QK)))
    wk_rope = jnp.tile(e_rope, (1, B_HEADS))
    w_k = jnp.concatenate([wk_nope, wk_rope, jnp.zeros((256 - KV_LORA - QK_ROPE, B_PAD), F32)], axis=0).astype(BF16)
    w_v = ukv[:, :, QK_NOPE:].reshape(KV_LORA, B_HEADS * V_HEAD).astype(BF16)
    pad_g = lambda g: jnp.tile(jnp.pad(g, (0, LANES - B_QK)), B_HEADS).reshape(1, B_PAD)
    idx = jnp.arange(256)
    bd = lambda n: (idx[:, None] // n == idx[None, :] // n).astype(BF16)
    ca, sua, sda = _rope_tables(seq, A_HEAD_DIM, 2, 0, 0)
    cb, sub, sdb = _rope_tables(seq, QK_ROPE, 1, QK_NOPE, LANES - B_QK)
    hl = jnp.arange(LANES)
    half = QK_ROPE // 2
    is_lo = (hl >= QK_NOPE) & (hl < QK_NOPE + half)
    is_hi = (hl >= QK_NOPE + half) & (hl < B_QK)
    partner = jnp.where(is_lo, hl + half, jnp.where(is_hi, hl - half, hl))
    has_partner = (is_lo | is_hi).astype(F32)

    def swap(a):
        a3 = a.reshape(a.shape[:-1] + (B_HEADS, LANES))
        return (jnp.take(a3, partner, axis=-1) * has_partner).astype(a.dtype).reshape(a.shape)
    bound = (B_QK ** 0.5) * LOG2E * jnp.max(jnp.abs(gq_b)) * jnp.max(jnp.abs(gk_b))
    shift = jnp.ceil(bound * 1.02) + 1.0
    lane = jnp.arange(LANES)
    qbias = jnp.where(lane == B_QK, 1.0, 0.0).astype(F32).reshape(1, LANES)
    kbias = jnp.where(lane == B_QK, -shift, 0.0).astype(F32).reshape(1, LANES)
    bound_a = (A_HEAD_DIM ** 0.5) * LOG2E * jnp.max(jnp.abs(gq_a)) * jnp.max(jnp.abs(gk_a))
    shift_a0 = jnp.ceil(bound_a * 1.02) + 1.0
    sink2 = sink_a * LOG2E
    shift_a = jnp.maximum(shift_a0, sink2).astype(BF16).astype(F32)
    qabias = jnp.where(lane[None, :] == A_HEAD_DIM, -shift_a[:, None], 0.0).reshape(A_HEADS, 1, LANES)
    kabias = jnp.where(lane == A_HEAD_DIM, 1.0, 0.0).astype(F32).reshape(1, LANES)
    rows = lambda v: jnp.repeat(v, WINDOW).reshape(A_KV_HEADS, 1, A_HEADS // A_KV_HEADS * WINDOW)
    return {
        "shift": shift, "qbias": qbias, "kbias": kbias,
        "shift_a": shift_a0, "qabias": qabias, "kabias": kabias,
        "sink_fast": rows(jnp.exp2(sink2 - shift_a)), "sink_safe": rows(sink2 - shift_a),
        "w_ada": w_ada, "b_ada": b_ada,
        "g_attn": g_attn.reshape(1, D_MODEL),
        "w_in": w_in_p, "w_gate": w_gate, "w_uq": w_uq_p, "w_k": w_k, "w_v": w_v,
        "w_uq_sw": swap(w_uq_p), "w_k_sw": swap(w_k),
        "gq_b_sw": swap(pad_g(gq_b)) * (B_QK ** -0.5 * LOG2E), "gk_b_sw": swap(pad_g(gk_b)), "sb": sub + sdb,
        "g_cq": g_cq.reshape(1, Q_LORA), "g_ckv": g_ckv.reshape(1, KV_LORA),
        "gq_a": (jnp.tile(gq_a, A_HEADS) * (A_HEAD_DIM ** -0.5 * LOG2E)).reshape(1, A_Q),
        "gk_a": jnp.tile(gk_a, A_KV_HEADS).reshape(1, A_KV),
        "gq_b": pad_g(gq_b) * (B_QK ** -0.5 * LOG2E), "gk_b": pad_g(gk_b),
        "bd64": bd(A_HEAD_DIM), "bd128": bd(LANES),
        "ca": ca, "sua": sua, "sda": sda, "cb": cb,
        "w_o_a": w_o_a.astype(BF16), "w_o_b": w_o_b.astype(BF16), "w_out": w_out.astype(BF16),
        "g_ffn": g_ffn.reshape(1, D_MODEL),
        "w_up": w_up.astype(BF16), "conv_w": conv_w, "conv_b": conv_b.reshape(1, 2 * D_FF),
        "w_down": w_down.astype(BF16),
    }


def _tile(s, pref):
    t = min(pref, s)
    assert s % t == 0 and t % WINDOW == 0, (s, t)
    return t


def _layer(x, c, w):
    b, s, _ = x.shape
    mod = _modulation(c, w["w_ada"], w["b_ada"]).reshape(b, 6, D_MODEL)
    qa, ka, va, qb, kb, vt = _pre_attention(x, mod, w, _tile(s, 1024))
    tw = _tile(s, 512)
    oa = lax.cond(w["shift_a"] <= MAX_FIXED_SHIFT,
                  lambda *a: _window_attention(*a, w["sink_fast"], tw, True),
                  lambda *a: _window_attention(*a, w["sink_safe"], tw, False), qa, ka, va)
    tq = _tile(s, 256)
    ob = lax.cond(w["shift"] <= MAX_FIXED_SHIFT,
                  lambda *a: _mla_attention(*a, tq, True), lambda *a: _mla_attention(*a, tq, False), qb, kb, vt)
    x1, h2 = _merge(x, mod, oa, ob, w, _tile(s, 1024))
    return _ffn(x1, mod, h2, w, _tile(s, 512))


def kernel(x_prompt, x_sample, c_prompt, c_sample, w_ada, b_ada, g_attn, w_in, gq_a, gk_a, sink_a, g_cq, w_uq,
           g_ckv, w_ukv, gq_b, gk_b, w_o_a, w_o_b, w_out, g_ffn, w_up, conv_w, conv_b, w_down):
    y_prompt, y_sample = x_prompt, x_sample
    for l in range(w_ada.shape[0]):
        p = (w_ada[l], b_ada[l], g_attn[l], w_in[l], gq_a[l], gk_a[l], sink_a[l], g_cq[l], w_uq[l], g_ckv[l],
             w_ukv[l], gq_b[l], gk_b[l], w_o_a[l], w_o_b[l], w_out[l], g_ffn[l], w_up[l], conv_w[l], conv_b[l],
             w_down[l])
        assert y_prompt.shape[1] == y_sample.shape[1]
        w = _prepare(y_prompt.shape[1], *p)
        y_prompt = _layer(y_prompt, c_prompt, w)
        y_sample = _layer(y_sample, c_sample, w)
    return (y_prompt, y_sample)
```

```python
import functools

import jax
import jax.numpy as jnp
from jax import lax
from jax.experimental import pallas as pl
from jax.experimental.pallas import tpu as pltpu

F32 = jnp.float32
BF16 = jnp.bfloat16

D_MODEL = 1024
A_HEADS = 8
A_KV_HEADS = 2
A_HEAD_DIM = 64
WINDOW = 128
B_HEADS = 8
Q_LORA = 256
KV_LORA = 128
QK_NOPE = 64
QK_ROPE = 32
V_HEAD = 64
D_FF = 2816
ROPE_THETA = 10000.0
EPS = 1e-6
NEG_INF = -1e30

A_Q = A_HEADS * A_HEAD_DIM
A_KV = A_KV_HEADS * A_HEAD_DIM
B_QK = QK_NOPE + QK_ROPE
LANES = 128
B_PAD = B_HEADS * LANES
VT_ROWS = 80

C_QA, C_KA, C_VA, C_CQ, C_CKV, C_KR, C_END = 0, 512, 640, 768, 1024, 1152, 1280

VMEM_LIMIT = 56 * 1024 * 1024
LOG2E = 1.4426950408889634
MAX_FIXED_SHIFT = 50.0
MLA_LOOKAHEAD = 4
WINDOW_LOOKAHEAD = 1
PRE_LOOKAHEAD = 1
PRE_SUB = 256


def _cparams(n_axes):
    return pltpu.CompilerParams(dimension_semantics=("arbitrary",) * n_axes, vmem_limit_bytes=VMEM_LIMIT)


def _dot(a, b):
    return jnp.dot(a, b, preferred_element_type=F32)


def _dot_nt(a, b):
    return lax.dot_general(a, b, (((1,), (1,)), ((), ())), preferred_element_type=F32)


def _mod_kernel(c_ref, w_ref, b_ref, o_ref):
    c = c_ref[...]
    sc = c * jax.nn.sigmoid(c)
    o_ref[...] = _dot(sc.astype(BF16), w_ref[...].astype(BF16)) + b_ref[...]


def _modulation(c, w_ada, b_ada):
    b = c.shape[0]
    n = w_ada.shape[1]
    tn = 1024
    return pl.pallas_call(
        _mod_kernel,
        grid=(n // tn,),
        in_specs=[
            pl.BlockSpec((b, D_MODEL), lambda j: (0, 0)),
            pl.BlockSpec((D_MODEL, tn), lambda j: (0, j)),
            pl.BlockSpec((1, tn), lambda j: (0, j)),
        ],
        out_specs=pl.BlockSpec((b, tn), lambda j: (0, j)),
        out_shape=jax.ShapeDtypeStruct((b, n), F32),
        compiler_params=_cparams(1),
        name="adaln_mod",
    )(c, w_ada, b_ada.reshape(1, n))


def _group_norm(x, bd, inv_n, gain):
    ss = _dot((x * x).astype(BF16), bd)
    return x * lax.rsqrt(ss * inv_n + EPS) * gain


def _rope(x, c, s_up, s_dn, shift):
    return x * c + pltpu.roll(x, LANES - shift, 1) * s_up + pltpu.roll(x, shift, 1) * s_dn


def _pre_kernel(*refs):
    n_sub = refs[0].shape[1] // PRE_SUB
    pending = {s: _pre_project(s, *refs) for s in range(min(PRE_LOOKAHEAD, n_sub))}
    for sub in range(n_sub):
        if sub + PRE_LOOKAHEAD < n_sub:
            pending[sub + PRE_LOOKAHEAD] = _pre_project(sub + PRE_LOOKAHEAD, *refs)
        _pre_finish(sub, pending.pop(sub), *refs)


def _pre_project(sub, x_ref, mod_ref, g_ref, w_in_ref, *_):
    x = x_ref[0, sub * PRE_SUB:(sub + 1) * PRE_SUB]
    sh1 = mod_ref[0, 0:1, :]
    sc1 = mod_ref[0, 1:2, :]
    h = x * lax.rsqrt(jnp.mean(x * x, axis=-1, keepdims=True) + EPS) * g_ref[...]
    hb = (h * (1.0 + sc1) + sh1).astype(BF16)
    return _dot(hb, w_in_ref[...])


def _pre_finish(sub, z, x_ref, mod_ref, g_ref, w_in_ref, w_uq_ref, w_uqs_ref, w_k_ref, w_ks_ref, w_v_ref, gcq_ref,
                gckv_ref, gqa_ref, gka_ref, gqb_ref, gqbs_ref, gkb_ref, gkbs_ref, bd64_ref, bd128_ref,
                qabias_ref, kabias_ref, qbias_ref, kbias_ref, ca_ref, sua_ref, sda_ref, cb_ref, sb_ref,
                qa_o, ka_o, va_o, qb_o, kb_o, vt_o):
    t = PRE_SUB
    rows = slice(sub * t, (sub + 1) * t)
    ca, sua, sda = ca_ref[rows], sua_ref[rows], sda_ref[rows]
    cb, sb = cb_ref[rows], sb_ref[rows]
    bd64 = bd64_ref[...]
    bd128 = bd128_ref[...]

    low_half = lax.broadcasted_iota(jnp.int32, (t, LANES), 1) < A_HEAD_DIM
    row = lax.broadcasted_iota(jnp.int32, (VT_ROWS - V_HEAD, LANES), 0)
    tail_a = jnp.where(row == 0, 1.0, 0.0).astype(BF16)

    def split_heads(pair):
        return jnp.where(low_half, pair, 0.0), jnp.where(low_half, pltpu.roll(pair, A_HEAD_DIM, 1), 0.0)

    zq = z[:, C_QA:C_KA]
    for c in range(2):
        xn = _group_norm(zq[:, 256 * c:256 * (c + 1)], bd64, 1.0 / A_HEAD_DIM, gqa_ref[:, 256 * c:256 * (c + 1)])
        for j in range(2):
            hq = 4 * c + 2 * j
            even, odd = split_heads(_rope(xn[:, LANES * j:LANES * (j + 1)], ca, sua, sda, 32))
            qa_o[0, hq, rows] = (even + qabias_ref[hq]).astype(BF16)
            qa_o[0, hq + 1, rows] = (odd + qabias_ref[hq + 1]).astype(BF16)
    zk = z[:, C_KA:C_VA]
    ssk = _dot((zk * zk).astype(BF16), bd64[0:LANES, 0:LANES])
    kn = zk * lax.rsqrt(ssk * (1.0 / A_HEAD_DIM) + EPS) * gka_ref[...]
    for hk, kh in enumerate(split_heads(_rope(kn, ca, sua, sda, 32))):
        ka_o[0, hk, rows] = (kh + kabias_ref[...]).astype(BF16)
    vat = z[:, C_VA:C_CQ].T
    for hk in range(A_KV_HEADS):
        for blk in range(t // WINDOW):
            ob = sub * (t // WINDOW) + blk
            va_o[0, hk, ob, 0:A_HEAD_DIM, :] = vat[A_HEAD_DIM * hk:A_HEAD_DIM * (hk + 1),
                                                   WINDOW * blk:WINDOW * (blk + 1)].astype(BF16)
            va_o[0, hk, ob, A_HEAD_DIM:VT_ROWS, :] = tail_a

    cq = z[:, C_CQ:C_CKV]
    cqn = (cq * lax.rsqrt(jnp.mean(cq * cq, axis=-1, keepdims=True) + EPS) * gcq_ref[...]).astype(BF16)
    ckv = z[:, C_CKV:C_KR]
    ckvn = (ckv * lax.rsqrt(jnp.mean(ckv * ckv, axis=-1, keepdims=True) + EPS) * gckv_ref[...]).astype(BF16)
    krp = z[:, C_KR:C_END].astype(BF16)

    k_lhs = jnp.concatenate([ckvn, krp], axis=1)
    qraw, qpar = _dot(cqn, w_uq_ref[...]), _dot(cqn, w_uqs_ref[...])
    kraw, kpar = _dot(k_lhs, w_k_ref[...]), _dot(k_lhs, w_ks_ref[...])
    for c in range(4):
        sl = slice(256 * c, 256 * (c + 1))
        rq = lax.rsqrt(_dot((qraw[:, sl] * qraw[:, sl]).astype(BF16), bd128) * (1.0 / B_QK) + EPS)
        rk = lax.rsqrt(_dot((kraw[:, sl] * kraw[:, sl]).astype(BF16), bd128) * (1.0 / B_QK) + EPS)
        qn, qp = qraw[:, sl] * rq * gqb_ref[:, sl], qpar[:, sl] * rq * gqbs_ref[:, sl]
        kn2, kp = kraw[:, sl] * rk * gkb_ref[:, sl], kpar[:, sl] * rk * gkbs_ref[:, sl]
        for j in range(2):
            hh = 2 * c + j
            ls = slice(LANES * j, LANES * (j + 1))
            qb_o[0, hh, rows] = (qn[:, ls] * cb + qp[:, ls] * sb + qbias_ref[...]).astype(BF16)
            kb_o[0, hh, rows] = (kn2[:, ls] * cb + kp[:, ls] * sb + kbias_ref[...]).astype(BF16)

    vv = _dot(ckvn, w_v_ref[...])
    vt = vv.T
    row = lax.broadcasted_iota(jnp.int32, (VT_ROWS - V_HEAD, t), 0)
    tail = jnp.where(row == 0, 1.0, 0.0).astype(BF16)
    for hh in range(B_HEADS):
        vt_o[0, hh, sub, 0:V_HEAD, :] = vt[V_HEAD * hh:V_HEAD * (hh + 1), :].astype(BF16)
        vt_o[0, hh, sub, V_HEAD:VT_ROWS, :] = tail


def _pre_attention(x, mod, w, t):
    b, s, _ = x.shape
    nt = s // t
    const2 = lambda i, bb: (0, 0)
    tok2 = lambda i, bb: (i, 0)
    in_specs = [
        pl.BlockSpec((1, t, D_MODEL), lambda i, bb: (bb, i, 0)),
        pl.BlockSpec((1, 6, D_MODEL), lambda i, bb: (bb, 0, 0)),
        pl.BlockSpec((1, D_MODEL), const2),
        pl.BlockSpec((D_MODEL, C_END), const2),
        pl.BlockSpec((Q_LORA, B_PAD), const2),
        pl.BlockSpec((Q_LORA, B_PAD), const2),
        pl.BlockSpec((256, B_PAD), const2),
        pl.BlockSpec((256, B_PAD), const2),
        pl.BlockSpec((KV_LORA, B_HEADS * V_HEAD), const2),
        pl.BlockSpec((1, Q_LORA), const2),
        pl.BlockSpec((1, KV_LORA), const2),
        pl.BlockSpec((1, A_Q), const2),
        pl.BlockSpec((1, A_KV), const2),
        pl.BlockSpec((1, B_PAD), const2),
        pl.BlockSpec((1, B_PAD), const2),
        pl.BlockSpec((1, B_PAD), const2),
        pl.BlockSpec((1, B_PAD), const2),
        pl.BlockSpec((256, 256), const2),
        pl.BlockSpec((256, 256), const2),
        pl.BlockSpec((A_HEADS, 1, LANES), lambda i, bb: (0, 0, 0)),
        pl.BlockSpec((1, LANES), const2),
        pl.BlockSpec((1, LANES), const2),
        pl.BlockSpec((1, LANES), const2),
    ] + [pl.BlockSpec((t, LANES), tok2)] * 5
    out_shape = [
        jax.ShapeDtypeStruct((b, A_HEADS, s, LANES), BF16),
        jax.ShapeDtypeStruct((b, A_KV_HEADS, s, LANES), BF16),
        jax.ShapeDtypeStruct((b, A_KV_HEADS, s // WINDOW, VT_ROWS, WINDOW), BF16),
        jax.ShapeDtypeStruct((b, B_HEADS, s, LANES), BF16),
        jax.ShapeDtypeStruct((b, B_HEADS, s, LANES), BF16),
        jax.ShapeDtypeStruct((b, B_HEADS, s // PRE_SUB, VT_ROWS, PRE_SUB), BF16),
    ]
    out_specs = [
        pl.BlockSpec((1, A_HEADS, t, LANES), lambda i, bb: (bb, 0, i, 0)),
        pl.BlockSpec((1, A_KV_HEADS, t, LANES), lambda i, bb: (bb, 0, i, 0)),
        pl.BlockSpec((1, A_KV_HEADS, t // WINDOW, VT_ROWS, WINDOW), lambda i, bb: (bb, 0, i, 0, 0)),
        pl.BlockSpec((1, B_HEADS, t, LANES), lambda i, bb: (bb, 0, i, 0)),
        pl.BlockSpec((1, B_HEADS, t, LANES), lambda i, bb: (bb, 0, i, 0)),
        pl.BlockSpec((1, B_HEADS, t // PRE_SUB, VT_ROWS, PRE_SUB), lambda i, bb: (bb, 0, i, 0, 0)),
    ]
    return pl.pallas_call(
        _pre_kernel,
        grid=(nt, b),
        in_specs=in_specs,
        out_specs=out_specs,
        out_shape=out_shape,
        compiler_params=_cparams(2),
        name="pre_attention",
    )(x, mod, w["g_attn"], w["w_in"], w["w_uq"], w["w_uq_sw"], w["w_k"], w["w_k_sw"], w["w_v"], w["g_cq"], w["g_ckv"],
      w["gq_a"], w["gk_a"], w["gq_b"], w["gq_b_sw"], w["gk_b"], w["gk_b_sw"], w["bd64"], w["bd128"],
      w["qabias"], w["kabias"], w["qbias"], w["kbias"],
      w["ca"], w["sua"], w["sda"], w["cb"], w["sb"])


def _window_kernel(r, fast, sink_ref, q_ref, kp_ref, km_ref, kn_ref, vp_ref, vm_ref, vn_ref, o_ref,
                   kcat, bias_scr, o_scr):
    i = pl.program_id(1)
    last_i = pl.num_programs(1) - 1
    blk = WINDOW
    grp = A_HEADS // A_KV_HEADS
    nq = grp * blk

    @pl.when(i == 0)
    def _():
        kk = lax.broadcasted_iota(jnp.int32, (3 * blk, nq), 0)
        qq = lax.broadcasted_iota(jnp.int32, (3 * blk, nq), 1) & (blk - 1)
        base = jnp.where(kk >= qq, jnp.where(kk <= qq + 2 * blk, 0.0, NEG_INF), NEG_INF)
        bias_scr[0] = base
        bias_scr[1] = jnp.where(kk >= blk, base, NEG_INF)
        bias_scr[2] = jnp.where(kk < 2 * blk, base, NEG_INF)

    for hk in range(A_KV_HEADS):
        kcat[hk, 0:blk] = kp_ref[0, hk]
        kcat[hk, blk:(r + 1) * blk] = km_ref[0, hk]
        kcat[hk, (r + 1) * blk:(r + 2) * blk] = kn_ref[0, hk]
    units = [(j, hk) for j in range(r) for hk in range(A_KV_HEADS)]

    def scores(u):
        j, hk = units[u]
        q4 = jnp.concatenate([q_ref[0, hk * grp + g, j * blk:(j + 1) * blk, :] for g in range(grp)], axis=0)
        return _dot_nt(kcat[hk, j * blk:(j + 3) * blk, :], q4)

    pending = {u: scores(u) for u in range(WINDOW_LOOKAHEAD)}
    for u, (j, hk) in enumerate(units):
        if u + WINDOW_LOOKAHEAD < len(units):
            pending[u + WINDOW_LOOKAHEAD] = scores(u + WINDOW_LOOKAHEAD)
        if j == 0:
            bias = bias_scr[jnp.where(i == 0, 1, 0)]
        elif j == r - 1:
            bias = bias_scr[jnp.where(i == last_i, 2, 0)]
        else:
            bias = bias_scr[0]
        st = pending.pop(u) + bias
        if fast:
            p = jnp.exp2(st).astype(BF16)
            sink_term = sink_ref[hk]
        else:
            srow = sink_ref[hk]
            m = jnp.maximum(jnp.max(st, axis=0, keepdims=True), srow)
            p = jnp.exp2(st - m).astype(BF16)
            sink_term = jnp.exp2(srow - m)
        vblk = [vp_ref[0, hk, 0]] + [vm_ref[0, hk, n] for n in range(r)] + [vn_ref[0, hk, 0]]
        acc = _dot(jnp.concatenate(vblk[j:j + 3], axis=1), p)
        ot = acc[0:A_HEAD_DIM] / (acc[A_HEAD_DIM:A_HEAD_DIM + 1] + sink_term)
        for g in range(grp):
            hq = hk * grp + g
            o_scr[hq * A_HEAD_DIM:(hq + 1) * A_HEAD_DIM, j * blk:(j + 1) * blk] = ot[:, g * blk:(g + 1) * blk]
    o_ref[0] = o_scr[...].T.astype(BF16)


def _window_attention(qa, ka, vat, sink_rows, tq, fast):
    b, _, s, _ = qa.shape
    r = tq // WINDOW
    nblk = s // WINDOW
    assert r >= 2
    grp = A_HEADS // A_KV_HEADS
    prev = lambda i: jnp.maximum(i * r - 1, 0)
    nxt = lambda i: jnp.minimum((i + 1) * r, nblk - 1)
    k_specs = [
        pl.BlockSpec((1, A_KV_HEADS, WINDOW, LANES), lambda bb, i: (bb, 0, prev(i), 0)),
        pl.BlockSpec((1, A_KV_HEADS, tq, LANES), lambda bb, i: (bb, 0, i, 0)),
        pl.BlockSpec((1, A_KV_HEADS, WINDOW, LANES), lambda bb, i: (bb, 0, nxt(i), 0)),
    ]
    v_specs = [
        pl.BlockSpec((1, A_KV_HEADS, 1, VT_ROWS, WINDOW), lambda bb, i: (bb, 0, prev(i), 0, 0)),
        pl.BlockSpec((1, A_KV_HEADS, r, VT_ROWS, WINDOW), lambda bb, i: (bb, 0, i, 0, 0)),
        pl.BlockSpec((1, A_KV_HEADS, 1, VT_ROWS, WINDOW), lambda bb, i: (bb, 0, nxt(i), 0, 0)),
    ]
    return pl.pallas_call(
        functools.partial(_window_kernel, r, fast),
        grid=(b, s // tq),
        in_specs=[pl.BlockSpec((A_KV_HEADS, 1, grp * WINDOW), lambda bb, i: (0, 0, 0)),
                  pl.BlockSpec((1, A_HEADS, tq, LANES), lambda bb, i: (bb, 0, i, 0))] + k_specs + v_specs,
        out_specs=pl.BlockSpec((1, tq, A_Q), lambda bb, i: (bb, i, 0)),
        out_shape=jax.ShapeDtypeStruct((b, s, A_Q), BF16),
        scratch_shapes=[pltpu.VMEM((A_KV_HEADS, tq + 2 * WINDOW, LANES), BF16),
                        pltpu.VMEM((3, 3 * WINDOW, grp * WINDOW), F32),
                        pltpu.VMEM((A_Q, tq), F32)],
        compiler_params=_cparams(2),
        name="window_attention_fast" if fast else "window_attention_safe",
    )(sink_rows, qa, ka, ka, ka, vat, vat, vat)


def _mla_safe_kernel(nc, q_ref, k_ref, vt_ref, o_ref, s_scr, o_scr):
    tk = vt_ref.shape[-1]
    tq = q_ref.shape[2]

    def head(h, carry):
        q = q_ref[0, h]

        def pass1(c, m):
            kc = k_ref[0, h, pl.ds(pl.multiple_of(c * tk, tk), tk), :]
            st = _dot_nt(kc, q)
            s_scr[c] = st
            return jnp.maximum(m, jnp.max(st, axis=0, keepdims=True))

        m = lax.fori_loop(0, nc, pass1, jnp.full((1, tq), -jnp.inf, F32))

        def pass2(c, acc):
            p = jnp.exp2(s_scr[c] - m).astype(BF16)
            return acc + _dot(vt_ref[0, h, c], p)

        acc = lax.fori_loop(0, nc, pass2, jnp.zeros((VT_ROWS, tq), F32))
        o_scr[pl.ds(pl.multiple_of(h * V_HEAD, V_HEAD), V_HEAD), :] = acc[0:V_HEAD] / acc[V_HEAD:V_HEAD + 1]
        return carry

    lax.fori_loop(0, B_HEADS, head, 0)
    o_ref[0] = o_scr[...].T.astype(BF16)


def _mla_fast_kernel(nc, q_ref, k_ref, vt_ref, o_ref, o_scr):
    tk = vt_ref.shape[-1]
    steps = [(h, c) for h in range(B_HEADS) for c in range(nc)]

    def scores(t):
        h, c = steps[t]
        return _dot_nt(k_ref[0, h, c * tk:(c + 1) * tk, :], q_ref[0, h])

    pending = {t: scores(t) for t in range(MLA_LOOKAHEAD)}
    acc = None
    for t, (h, c) in enumerate(steps):
        if t + MLA_LOOKAHEAD < len(steps):
            pending[t + MLA_LOOKAHEAD] = scores(t + MLA_LOOKAHEAD)
        p = jnp.exp2(pending.pop(t)).astype(BF16)
        d = _dot(vt_ref[0, h, c], p)
        acc = d if c == 0 else acc + d
        if c == nc - 1:
            o_scr[h * V_HEAD:(h + 1) * V_HEAD, :] = acc[0:V_HEAD] / acc[V_HEAD:V_HEAD + 1]
    o_ref[0] = o_scr[...].T.astype(BF16)


def _mla_attention(qb, kb, vt, tq, fast):
    b, _, s, _ = qb.shape
    nc, tk = vt.shape[2], vt.shape[4]
    scratch = [pltpu.VMEM((B_HEADS * V_HEAD, tq), F32)]
    if fast:
        body = functools.partial(_mla_fast_kernel, nc)
    else:
        body = functools.partial(_mla_safe_kernel, nc)
        scratch = [pltpu.VMEM((nc, tk, tq), F32)] + scratch
    return pl.pallas_call(
        body,
        grid=(b, s // tq),
        in_specs=[
            pl.BlockSpec((1, B_HEADS, tq, LANES), lambda bb, i: (bb, 0, i, 0)),
            pl.BlockSpec((1, B_HEADS, s, LANES), lambda bb, i: (bb, 0, 0, 0)),
            pl.BlockSpec((1, B_HEADS, nc, VT_ROWS, tk), lambda bb, i: (bb, 0, 0, 0, 0)),
        ],
        out_specs=pl.BlockSpec((1, tq, B_HEADS * V_HEAD), lambda bb, i: (bb, i, 0)),
        out_shape=jax.ShapeDtypeStruct((b, s, B_HEADS * V_HEAD), BF16),
        scratch_shapes=scratch,
        compiler_params=_cparams(2),
        name="mla_attention_fast" if fast else "mla_attention_safe",
    )(qb, kb, vt)


def _merge_kernel(x_ref, mod_ref, oa_ref, ob_ref, g1_ref, wg_ref, woa_ref, wob_ref, wout_ref, g_ref,
                  x1_o, h2_o):
    sh1 = mod_ref[0, 0:1, :]
    sc1 = mod_ref[0, 1:2, :]
    gt1 = mod_ref[0, 2:3, :]
    sh2 = mod_ref[0, 3:4, :]
    sc2 = mod_ref[0, 4:5, :]
    n_sub = x_ref.shape[1] // PRE_SUB

    def gated_mix(sub):
        rows = slice(sub * PRE_SUB, (sub + 1) * PRE_SUB)
        x = x_ref[0, rows]
        h = x * lax.rsqrt(jnp.mean(x * x, axis=-1, keepdims=True) + EPS) * g1_ref[...]
        hb = (h * (1.0 + sc1) + sh1).astype(BF16)
        mix = (jax.nn.sigmoid(_dot(hb, wg_ref[:, 0:D_MODEL])) * _dot(oa_ref[0, rows], woa_ref[...])
               + jax.nn.sigmoid(_dot(hb, wg_ref[:, D_MODEL:2 * D_MODEL])) * _dot(ob_ref[0, rows], wob_ref[...]))
        return x, mix.astype(BF16)

    cur = gated_mix(0)
    for sub in range(n_sub):
        nxt = gated_mix(sub + 1) if sub + 1 < n_sub else None
        rows = slice(sub * PRE_SUB, (sub + 1) * PRE_SUB)
        x, mix = cur
        x1 = x + gt1 * _dot(mix, wout_ref[...])
        x1_o[0, rows] = x1
        h2 = x1 * lax.rsqrt(jnp.mean(x1 * x1, axis=-1, keepdims=True) + EPS) * g_ref[...]
        h2_o[0, rows] = (h2 * (1.0 + sc2) + sh2).astype(BF16)
        cur = nxt


def _merge(x, mod, oa, ob, w, t):
    b, s, _ = x.shape
    tok = lambda i, bb: (bb, i, 0)
    const2 = lambda i, bb: (0, 0)
    return pl.pallas_call(
        _merge_kernel,
        grid=(s // t, b),
        in_specs=[
            pl.BlockSpec((1, t, D_MODEL), tok),
            pl.BlockSpec((1, 6, D_MODEL), lambda i, bb: (bb, 0, 0)),
            pl.BlockSpec((1, t, A_Q), tok),
            pl.BlockSpec((1, t, B_HEADS * V_HEAD), tok),
            pl.BlockSpec((1, D_MODEL), const2),
            pl.BlockSpec((D_MODEL, 2 * D_MODEL), const2),
            pl.BlockSpec((A_Q, D_MODEL), const2),
            pl.BlockSpec((B_HEADS * V_HEAD, D_MODEL), const2),
            pl.BlockSpec((D_MODEL, D_MODEL), const2),
            pl.BlockSpec((1, D_MODEL), const2),
        ],
        out_specs=[pl.BlockSpec((1, t, D_MODEL), tok), pl.BlockSpec((1, t, D_MODEL), tok)],
        out_shape=[jax.ShapeDtypeStruct((b, s, D_MODEL), F32), jax.ShapeDtypeStruct((b, s, D_MODEL), BF16)],
        compiler_params=_cparams(2),
        name="merge_outproj",
    )(x, mod, oa, ob, w["g_attn"], w["w_gate"], w["w_o_a"], w["w_o_b"], w["w_out"], w["g_ffn"])


MXU_DIM = 256
FF_SPLIT = (D_FF // MXU_DIM + 1) // 2 * MXU_DIM
FF_CHUNKS = ((0, FF_SPLIT), (FF_SPLIT, D_FF))
HALO = 16
FF_ROW_SPLIT = 2


def _ffn_kernel(x1_ref, mod_ref, hp_ref, hm_ref, hn_ref, wup_ref, cw_ref, cb_ref, wdn_ref, y_o,
                hcat, *bufs):
    u_scr = [bufs[3 * hf:3 * hf + 2] for hf in range(FF_ROW_SPLIT)]
    act_scr = [bufs[3 * hf + 2] for hf in range(FF_ROW_SPLIT)]
    i = pl.program_id(0)
    last = pl.num_programs(0) - 1
    t = hm_ref.shape[1]
    gt2 = mod_ref[0, 5:6, :]
    hcat[0:HALO] = jnp.where(i > 0, hp_ref[0], jnp.zeros_like(hp_ref[0]))
    hcat[HALO:HALO + t] = hm_ref[0]
    hcat[HALO + t:HALO + t + HALO] = jnp.where(i < last, hn_ref[0], jnp.zeros_like(hn_ref[0]))
    th = t // FF_ROW_SPLIT
    n_pieces = D_FF // MXU_DIM
    units = [(hf, p) for p in range(n_pieces) for hf in range(FF_ROW_SPLIT)]
    chunk_end = {c1 // MXU_DIM - 1: (c0, c1) for c0, c1 in FF_CHUNKS}

    def up_piece(hf, p):
        hc = hcat[hf * th:hf * th + th + 2 * HALO, :]
        for br, off in enumerate((0, D_FF)):
            lo = off + p * MXU_DIM
            u_scr[hf][br][...] = _dot(hc, wup_ref[:, lo:lo + MXU_DIM])

    def finish_piece(hf, p):
        branch = []
        for br, off in enumerate((0, D_FF)):
            lo = off + p * MXU_DIM
            u = cb_ref[:, lo:lo + MXU_DIM]
            for j in range(3):
                u = u + u_scr[hf][br][pl.ds(HALO - 1 + j, th), :] * cw_ref[j:j + 1, lo:lo + MXU_DIM]
            branch.append(u)
        ua, ug = branch
        act_scr[hf][:, p * MXU_DIM:(p + 1) * MXU_DIM] = (ug * jax.nn.sigmoid(ug) * ua).astype(BF16)

    acc = [None] * FF_ROW_SPLIT
    up_piece(*units[0])
    for k, (hf, p) in enumerate(units):
        if k + 1 < len(units):
            up_piece(*units[k + 1])
        finish_piece(hf, p)
        if p in chunk_end:
            c0, c1 = chunk_end[p]
            d = _dot(act_scr[hf][:, c0:c1], wdn_ref[c0:c1, :])
            acc[hf] = d if acc[hf] is None else acc[hf] + d
    for hf in range(FF_ROW_SPLIT):
        rows = slice(hf * th, (hf + 1) * th)
        y_o[0, rows] = x1_ref[0, rows] + gt2 * acc[hf]


def _ffn(x1, mod, h2, w, t):
    b, s, _ = x1.shape
    r = t // HALO
    nh = s // HALO
    tok = lambda i, bb: (bb, i, 0)
    const2 = lambda i, bb: (0, 0)
    return pl.pallas_call(
        _ffn_kernel,
        grid=(s // t, b),
        in_specs=[
            pl.BlockSpec((1, t, D_MODEL), tok),
            pl.BlockSpec((1, 6, D_MODEL), lambda i, bb: (bb, 0, 0)),
            pl.BlockSpec((1, HALO, D_MODEL), lambda i, bb: (bb, jnp.maximum(i * r - 1, 0), 0)),
            pl.BlockSpec((1, t, D_MODEL), tok),
            pl.BlockSpec((1, HALO, D_MODEL), lambda i, bb: (bb, jnp.minimum((i + 1) * r, nh - 1), 0)),
            pl.BlockSpec((D_MODEL, 2 * D_FF), const2, pipeline_mode=pl.Buffered(1)),
            pl.BlockSpec((3, 2 * D_FF), const2),
            pl.BlockSpec((1, 2 * D_FF), const2),
            pl.BlockSpec((D_FF, D_MODEL), const2, pipeline_mode=pl.Buffered(1)),
        ],
        out_specs=pl.BlockSpec((1, t, D_MODEL), tok),
        out_shape=jax.ShapeDtypeStruct((b, s, D_MODEL), F32),
        scratch_shapes=[
            pltpu.VMEM((t + 2 * HALO, D_MODEL), BF16),
        ] + [pltpu.VMEM((t // FF_ROW_SPLIT + 2 * HALO, MXU_DIM), F32),
             pltpu.VMEM((t // FF_ROW_SPLIT + 2 * HALO, MXU_DIM), F32),
             pltpu.VMEM((t // FF_ROW_SPLIT, D_FF), BF16)] * FF_ROW_SPLIT,
        compiler_params=_cparams(2),
        name="conv_ffn",
    )(x1, mod, h2, h2, h2, w["w_up"], w["conv_w"], w["conv_b"], w["w_down"])


def _rope_tables(seq, dim, n_rep, lead, pad):
    half = dim // 2
    inv = 1.0 / (ROPE_THETA ** (jnp.arange(0, dim, 2, dtype=F32) / dim))
    ang = jnp.arange(seq, dtype=F32)[:, None] * inv[None, :]
    cos, sin = jnp.cos(ang), jnp.sin(ang)
    zero = jnp.zeros_like(sin)
    c = jnp.concatenate([jnp.ones((seq, lead), F32)] + [cos, cos] * n_rep + [jnp.zeros((seq, pad), F32)], axis=1)
    s_up = jnp.concatenate([jnp.zeros((seq, lead), F32)] + [-sin, zero] * n_rep + [jnp.zeros((seq, pad), F32)], axis=1)
    s_dn = jnp.concatenate([jnp.zeros((seq, lead), F32)] + [zero, sin] * n_rep + [jnp.zeros((seq, pad), F32)], axis=1)
    del half
    return c, s_up, s_dn


def _prepare(seq, w_ada, b_ada, g_attn, w_in, gq_a, gk_a, sink_a, g_cq, w_uq, g_ckv, w_ukv, gq_b, gk_b,
             w_o_a, w_o_b, w_out, g_ffn, w_up, conv_w, conv_b, w_down):
    kr_lo = A_Q + 2 * A_KV + Q_LORA + KV_LORA
    kr_hi = kr_lo + QK_ROPE
    w_in_p = jnp.concatenate([w_in[:, :kr_hi], jnp.zeros((D_MODEL, LANES - QK_ROPE), F32)], axis=1).astype(BF16)
    w_gate = w_in[:, kr_hi:].astype(BF16)
    pad_h = lambda a: jnp.pad(a, ((0, 0), (0, 0), (0, LANES - a.shape[2]))).reshape(a.shape[0], B_PAD)
    w_uq_p = pad_h(w_uq.reshape(Q_LORA, B_HEADS, B_QK)).astype(BF16)
    ukv = w_ukv.reshape(KV_LORA, B_HEADS, QK_NOPE + V_HEAD)
    wk_nope = pad_h(ukv[:, :, :QK_NOPE])
    eye = jnp.eye(QK_ROPE, dtype=F32)
    e_rope = jnp.pad(eye, ((0, 0), (QK_NOPE, LANES - B_QK)))
    wk_rope = jnp.tile(e_rope, (1, B_HEADS))
    w_k = jnp.concatenate([wk_nope, wk_rope, jnp.zeros((256 - KV_LORA - QK_ROPE, B_PAD), F32)], axis=0).astype(BF16)
    w_v = ukv[:, :, QK_NOPE:].reshape(KV_LORA, B_HEADS * V_HEAD).astype(BF16)
    pad_g = lambda g: jnp.tile(jnp.pad(g, (0, LANES - B_QK)), B_HEADS).reshape(1, B_PAD)
    idx = jnp.arange(256)
    bd = lambda n: (idx[:, None] // n == idx[None, :] // n).astype(BF16)
    ca, sua, sda = _rope_tables(seq, A_HEAD_DIM, 2, 0, 0)
    cb, sub, sdb = _rope_tables(seq, QK_ROPE, 1, QK_NOPE, LANES - B_QK)
    hl = jnp.arange(LANES)
    half = QK_ROPE // 2
    is_lo = (hl >= QK_NOPE) & (hl < QK_NOPE + half)
    is_hi = (hl >= QK_NOPE + half) & (hl < B_QK)
    partner = jnp.where(is_lo, hl + half, jnp.where(is_hi, hl - half, hl))
    has_partner = (is_lo | is_hi).astype(F32)

    def swap(a):
        a3 = a.reshape(a.shape[:-1] + (B_HEADS, LANES))
        return (jnp.take(a3, partner, axis=-1) * has_partner).astype(a.dtype).reshape(a.shape)
    bound = (B_QK ** 0.5) * LOG2E * jnp.max(jnp.abs(gq_b)) * jnp.max(jnp.abs(gk_b))
    shift = jnp.ceil(bound * 1.02) + 1.0
    lane = jnp.arange(LANES)
    qbias = jnp.where(lane == B_QK, 1.0, 0.0).astype(F32).reshape(1, LANES)
    kbias = jnp.where(lane == B_QK, -shift, 0.0).astype(F32).reshape(1, LANES)
    bound_a = (A_HEAD_DIM ** 0.5) * LOG2E * jnp.max(jnp.abs(gq_a)) * jnp.max(jnp.abs(gk_a))
    shift_a0 = jnp.ceil(bound_a * 1.02) + 1.0
    sink2 = sink_a * LOG2E
    shift_a = jnp.maximum(shift_a0, sink2).astype(BF16).astype(F32)
    qabias = jnp.where(lane[None, :] == A_HEAD_DIM, -shift_a[:, None], 0.0).reshape(A_HEADS, 1, LANES)
    kabias = jnp.where(lane == A_HEAD_DIM, 1.0, 0.0).astype(F32).reshape(1, LANES)
    rows = lambda v: jnp.repeat(v, WINDOW).reshape(A_KV_HEADS, 1, A_HEADS // A_KV_HEADS * WINDOW)
    return {
        "shift": shift, "qbias": qbias, "kbias": kbias,
        "shift_a": shift_a0, "qabias": qabias, "kabias": kabias,
        "sink_fast": rows(jnp.exp2(sink2 - shift_a)), "sink_safe": rows(sink2 - shift_a),
        "w_ada": w_ada, "b_ada": b_ada,
        "g_attn": g_attn.reshape(1, D_MODEL),
        "w_in": w_in_p, "w_gate": w_gate, "w_uq": w_uq_p, "w_k": w_k, "w_v": w_v,
        "w_uq_sw": swap(w_uq_p), "w_k_sw": swap(w_k),
        "gq_b_sw": swap(pad_g(gq_b)) * (B_QK ** -0.5 * LOG2E), "gk_b_sw": swap(pad_g(gk_b)), "sb": sub + sdb,
        "g_cq": g_cq.reshape(1, Q_LORA), "g_ckv": g_ckv.reshape(1, KV_LORA),
        "gq_a": (jnp.tile(gq_a, A_HEADS) * (A_HEAD_DIM ** -0.5 * LOG2E)).reshape(1, A_Q),
        "gk_a": jnp.tile(gk_a, A_KV_HEADS).reshape(1, A_KV),
        "gq_b": pad_g(gq_b) * (B_QK ** -0.5 * LOG2E), "gk_b": pad_g(gk_b),
        "bd64": bd(A_HEAD_DIM), "bd128": bd(LANES),
        "ca": ca, "sua": sua, "sda": sda, "cb": cb,
        "w_o_a": w_o_a.astype(BF16), "w_o_b": w_o_b.astype(BF16), "w_out": w_out.astype(BF16),
        "g_ffn": g_ffn.reshape(1, D_MODEL),
        "w_up": w_up.astype(BF16), "conv_w": conv_w, "conv_b": conv_b.reshape(1, 2 * D_FF),
        "w_down": w_down.astype(BF16),
    }


def _tile(s, pref):
    t = min(pref, s)
    assert s % t == 0 and t % WINDOW == 0, (s, t)
    return t


def _layer(x, c, w):
    b, s, _ = x.shape
    mod = _modulation(c, w["w_ada"], w["b_ada"]).reshape(b, 6, D_MODEL)
    qa, ka, va, qb, kb, vt = _pre_attention(x, mod, w, _tile(s, 1024))
    tw = _tile(s, 512)
    oa = lax.cond(w["shift_a"] <= MAX_FIXED_SHIFT,
                  lambda *a: _window_attention(*a, w["sink_fast"], tw, True),
                  lambda *a: _window_attention(*a, w["sink_safe"], tw, False), qa, ka, va)
    tq = _tile(s, 256)
    ob = lax.cond(w["shift"] <= MAX_FIXED_SHIFT,
                  lambda *a: _mla_attention(*a, tq, True), lambda *a: _mla_attention(*a, tq, False), qb, kb, vt)
    x1, h2 = _merge(x, mod, oa, ob, w, _tile(s, 1024))
    return _ffn(x1, mod, h2, w, _tile(s, 512))


def kernel(x_prompt, x_sample, c_prompt, c_sample, w_ada, b_ada, g_attn, w_in, gq_a, gk_a, sink_a, g_cq, w_uq,
           g_ckv, w_ukv, gq_b, gk_b, w_o_a, w_o_b, w_out, g_ffn, w_up, conv_w, conv_b, w_down):
    y_prompt, y_sample = x_prompt, x_sample
    for l in range(w_ada.shape[0]):
        p = (w_ada[l], b_ada[l], g_attn[l], w_in[l], gq_a[l], gk_a[l], sink_a[l], g_cq[l], w_uq[l], g_ckv[l],
             w_ukv[l], gq_b[l], gk_b[l], w_o_a[l], w_o_b[l], w_out[l], g_ffn[l], w_up[l], conv_w[l], conv_b[l],
             w_down[l])
        assert y_prompt.shape[1] == y_sample.shape[1]
        w = _prepare(y_prompt.shape[1], *p)
        y_prompt = _layer(y_prompt, c_prompt, w)
        y_sample = _layer(y_sample, c_sample, w)
    return (y_prompt, y_sample)
```
